```python
import math
import jax
import jax.numpy as jnp
from jax import lax
import numpy as np

D_MODEL = 2048
BATCH = 8
SEQ = 2048
DEPTH = 2
DEC_BATCH = 128
DEC_SEQ = 8
PAST_LEN = 16384
PAGE_SIZE = 128

N_BRANCH = 4
BRANCH = D_MODEL // 4
MLA_NOPE = 64
MLA_ROPE = 32
MLA_VH = 64
MLA_H = BRANCH // MLA_VH
Q_LORA = 3 * D_MODEL // 16
KV_LORA = D_MODEL // 16
MLA_THETA = 10000.0
MLA_SCALE = 1.0 / math.sqrt(MLA_NOPE + MLA_ROPE)
CONV_CH = BRANCH
CONV_W = 31
POOL_CH = BRANCH
POOL_WINDOWS = (2, 4, 8, 16)
POOL_GROUP = POOL_CH // len(POOL_WINDOWS)
POOL_PREV = max(POOL_WINDOWS) - 1
DIFF_DH = 64
DIFF_VD = 2 * DIFF_DH
DIFF_H = BRANCH // DIFF_VD
DIFF_KV = 2
DIFF_REP = DIFF_H // DIFF_KV
DIFF_SCALE = 1.0 / math.sqrt(DIFF_DH)
ROT_DIM = DIFF_DH // 4
ROPE_THETA = 500000.0
N_GROUPS = 4
E_PER_GROUP = 4
N_EXPERTS = N_GROUPS * E_PER_GROUP
EXPERT_FF = D_MODEL // 4
TOP_K = 2
Q_BLOCK = 128
EPS = 1e-6
NEG_INF = -1e30
IN_SPLITS = (Q_LORA, KV_LORA, MLA_ROPE,
             2 * CONV_CH,
             POOL_CH,
             DIFF_H * 2 * DIFF_DH, DIFF_KV * 2 * DIFF_DH, DIFF_KV * DIFF_VD,
             N_BRANCH * D_MODEL)
IN_COLS = sum(IN_SPLITS)

kernel_name = 'hybrid_mla_conv_pool_diffattn_hmoe_step'


def rms_norm(x, g, eps=EPS):
    xf = x.astype(jnp.float32)
    y = xf * lax.rsqrt(jnp.mean(xf * xf, axis=-1, keepdims=True) + eps)
    return (y * g.astype(jnp.float32)).astype(x.dtype)


def layer_norm(x, g, b, eps=1e-5):
    xf = x.astype(jnp.float32)
    mu = jnp.mean(xf, axis=-1, keepdims=True)
    var = jnp.mean(jnp.square(xf - mu), axis=-1, keepdims=True)
    return ((xf - mu) * lax.rsqrt(var + eps) * g.astype(jnp.float32) + b.astype(jnp.float32)).astype(x.dtype)


def rope_cos_sin(pos, dim, theta):
    inv = 1.0 / (theta ** (jnp.arange(0, dim, 2, dtype=jnp.float32) / dim))
    ang = pos.astype(jnp.float32)[:, None] * inv[None, :]
    return jnp.cos(ang), jnp.sin(ang)


def apply_rope(x, cos, sin):
    half = x.shape[-1] // 2
    shape = (1, cos.shape[0]) + (1,) * (x.ndim - 3) + (half,)
    c, s = cos.reshape(shape), sin.reshape(shape)
    xf = x.astype(jnp.float32)
    x1, x2 = xf[..., :half], xf[..., half:]
    return jnp.concatenate([x1 * c - x2 * s, x1 * s + x2 * c], axis=-1).astype(x.dtype)


def partial_rope(x, cos, sin):
    return jnp.concatenate([apply_rope(x[..., :ROT_DIM], cos, sin), x[..., ROT_DIM:]], axis=-1)


def split_columns(z):
    out, start = [], 0
    for n in IN_SPLITS:
        out.append(z[..., start:start + n])
        start += n
    return out


def causal_block_attention(q, k, v, scale):
    B, T = q.shape[0], q.shape[1]
    nb = T // Q_BLOCK
    q_blocks = jnp.moveaxis(q.reshape((B, nb, Q_BLOCK) + q.shape[2:]), 1, 0)
    k_pos = jnp.arange(T)

    def one_block(args):
        qb, start = args
        s = jnp.einsum('bqgrmd,bkgmd->bgrmqk', qb, k, preferred_element_type=jnp.float32) * scale
        q_pos = start + jnp.arange(Q_BLOCK)
        s = jnp.where(k_pos[None, :] <= q_pos[:, None], s, NEG_INF)
        p = jax.nn.softmax(s, axis=-1).astype(v.dtype)
        return jnp.einsum('bgrmqk,bkgv->bqgrmv', p, v)

    o = lax.map(one_block, (q_blocks, jnp.arange(nb) * Q_BLOCK))
    return jnp.moveaxis(o, 0, 1).reshape((B, T) + o.shape[3:])


def online_update(carry, s, v):
    m, l, acc = carry
    m_new = jnp.maximum(m, jnp.max(s, axis=-1))
    corr = jnp.exp(m - m_new)
    p = jnp.exp(s - m_new[..., None])
    l = l * corr + jnp.sum(p, axis=-1)
    acc = acc * corr[..., None] + jnp.einsum('bgrmqk,bkgv->bgrmqv', p.astype(v.dtype), v,
                                             preferred_element_type=jnp.float32)
    return (m_new, l, acc)


def paged_attention(q, k_new, v_new, pools, layer, rows_to_kv, page_table, scale):
    Bd, Tn, G, R, M = q.shape[:5]
    Dv = v_new.shape[-1]

    def score(kb):
        return jnp.einsum('bqgrmd,bkgmd->bgrmqk', q, kb, preferred_element_type=jnp.float32) * scale

    def page_step(carry, phys):
        kb, vb = rows_to_kv(*[pool[layer, phys] for pool in pools])
        return online_update(carry, score(kb), vb), None

    init = (jnp.full((Bd, G, R, M, Tn), NEG_INF, jnp.float32),
            jnp.zeros((Bd, G, R, M, Tn), jnp.float32),
            jnp.zeros((Bd, G, R, M, Tn, Dv), jnp.float32))
    carry, _ = lax.scan(page_step, init, page_table.T)
    causal = jnp.arange(Tn)[None, :] <= jnp.arange(Tn)[:, None]
    _, l, acc = online_update(carry, jnp.where(causal, score(k_new), NEG_INF), v_new)
    o = acc / l[..., None]
    return jnp.transpose(o, (0, 4, 1, 2, 3, 5)).astype(v_new.dtype)


def mla_rows_to_kv(rows):
    return rows[:, :, None, None, :], rows[:, :, None, :KV_LORA]


def diff_rows_to_kv(k_rows, v_rows):
    return k_rows, v_rows


def token_mixer(h, pos, conv_prev, pool_prev, attend_mla, attend_diff, p, lam_init):
    B, T, _ = h.shape
    z = jnp.einsum('btd,dc->btc', h, p['w_in'])
    q_c, kv_c, k_rope, glu_in, pool_in, dq, dk, dv, gate_logits = split_columns(z)

    cos_m, sin_m = rope_cos_sin(pos, MLA_ROPE, MLA_THETA)
    q = jnp.einsum('btr,rhe->bthe', rms_norm(q_c, p['mla_q_norm']), p['mla_w_uq'])
    q_pe = apply_rope(q[..., MLA_NOPE:], cos_m, sin_m)
    q_lat = jnp.einsum('bthn,rhn->bthr', q[..., :MLA_NOPE], p['mla_w_uk'])
    c_kv = rms_norm(kv_c, p['mla_kv_norm'])
    k_pe = apply_rope(k_rope, cos_m, sin_m)
    mla_rows = jnp.concatenate([c_kv, k_pe], axis=-1)
    mla_q = jnp.concatenate([q_lat, q_pe], axis=-1)[:, :, None, :, None, :]
    o_lat = attend_mla(mla_q, mla_rows[:, :, None, None, :], c_kv[:, :, None, :], MLA_SCALE)
    a_out = jnp.einsum('bthr,rhv->bthv', o_lat[:, :, 0, :, 0, :], p['mla_w_uv']).reshape(B, T, BRANCH)

    glu = glu_in[..., :CONV_CH] * jax.nn.sigmoid(glu_in[..., CONV_CH:])
    conv_seq = jnp.concatenate([conv_prev.astype(glu.dtype), glu], axis=1)
    conv = lax.conv_general_dilated(conv_seq, p['conv_w'][:, None, :].astype(glu.dtype), window_strides=(1,),
                                    padding='VALID', dimension_numbers=('NWC', 'WIO', 'NWC'),
                                    feature_group_count=CONV_CH) + p['conv_b'].astype(glu.dtype)
    b_out = jax.nn.silu(layer_norm(conv, p['conv_ln_g'], p['conv_ln_b']))
    new_conv = conv_seq[:, -(CONV_W - 1):]

    pool_seq = jnp.concatenate([pool_prev.astype(pool_in.dtype), pool_in], axis=1)
    csum = jnp.pad(jnp.cumsum(pool_seq.astype(jnp.float32), axis=1), ((0, 0), (1, 0), (0, 0)))
    ends = csum[:, POOL_PREV + 1:]
    pooled = []
    for g, w in enumerate(POOL_WINDOWS):
        sl = slice(g * POOL_GROUP, (g + 1) * POOL_GROUP)
        start = csum[:, POOL_PREV + 1 - w:POOL_PREV + 1 - w + T, sl]
        count = jnp.minimum(w, pos + 1).astype(jnp.float32)[None, :, None]
        pooled.append((ends[..., sl] - start) / count)
    pooled = (jnp.concatenate(pooled, axis=-1).astype(pool_in.dtype) - pool_in).reshape(B, T, len(POOL_WINDOWS), POOL_GROUP)
    c_out = jnp.einsum('btgc,gce->btge', pooled, p['pool_w']).reshape(B, T, POOL_CH) * p['pool_scale']
    new_pool = pool_seq[:, -POOL_PREV:]

    cos_d, sin_d = rope_cos_sin(pos, ROT_DIM, ROPE_THETA)
    dq = partial_rope(dq.reshape(B, T, DIFF_KV, DIFF_REP, 2, DIFF_DH), cos_d, sin_d)
    dk = partial_rope(dk.reshape(B, T, DIFF_KV, 2, DIFF_DH), cos_d, sin_d)
    dv = dv.reshape(B, T, DIFF_KV, DIFF_VD)
    o = attend_diff(dq, dk, dv, DIFF_SCALE)
    lam = (jnp.exp(jnp.sum(p['lq1'].astype(jnp.float32) * p['lk1'].astype(jnp.float32)))
           - jnp.exp(jnp.sum(p['lq2'].astype(jnp.float32) * p['lk2'].astype(jnp.float32))) + lam_init).astype(o.dtype)
    diff = o[..., 0, :] - lam * o[..., 1, :]
    d_out = (rms_norm(diff, p['diff_subln'], 1e-5) * (1.0 - lam_init)).reshape(B, T, BRANCH)

    branches = jnp.stack([a_out, b_out, c_out, d_out], axis=2)
    up = jnp.einsum('btnc,ncd->btnd', branches, p['w_branch'])
    gates = jax.nn.sigmoid(gate_logits.reshape(B, T, N_BRANCH, D_MODEL))
    merged = jnp.sum(gates * up, axis=2)
    out = jnp.einsum('btd,de->bte', merged, p['w_out'])
    return out, (mla_rows, dk, dv, new_conv, new_pool)


def hier_moe(h, p):
    B, T, D = h.shape
    t = h.reshape(B * T, D)
    g_logits = jnp.einsum('td,dg->tg', t, p['router_g_w'], preferred_element_type=jnp.float32) + p['router_g_b']
    g_prob = jax.nn.softmax(g_logits, axis=-1)
    _, g_idx = lax.top_k(g_logits, 1)
    g_w = jnp.take_along_axis(g_prob, g_idx, axis=-1)
    e_all = jnp.einsum('td,gde->tge', t, p['router_e_w'], preferred_element_type=jnp.float32) + p['router_e_b']
    e_logits = jnp.take_along_axis(e_all, g_idx[:, :, None], axis=1)[:, 0]
    top_v, top_i = lax.top_k(e_logits, TOP_K)
    top_w = jax.nn.softmax(top_v, axis=-1) * g_w
    expert_idx = g_idx * E_PER_GROUP + top_i
    combine = jnp.sum(jax.nn.one_hot(expert_idx, N_EXPERTS, dtype=jnp.float32) * top_w[..., None], axis=1)
    hid = jax.nn.silu(jnp.einsum('td,edf->tef', t, p['w_gate'])) * jnp.einsum('td,edf->tef', t, p['w_up'])
    hid = hid * combine[..., None].astype(hid.dtype)
    return jnp.einsum('tef,efd->td', hid, p['w_down']).reshape(B, T, D)


def run_trunk(x, pos, conv_state, pool_state, attend_for_layer, layers, final_norm_g):
    news = []
    for l in range(DEPTH):
        p = layers[l]
        attend_mla, attend_diff = attend_for_layer(l)
        lam_init = 0.8 - 0.6 * math.exp(-0.3 * l)
        mix, new = token_mixer(rms_norm(x, p['norm_mix']), pos, conv_state[l], pool_state[l],
                               attend_mla, attend_diff, p, lam_init)
        x = x + mix
        x = x + hier_moe(rms_norm(x, p['norm_ffn']), p)
        news.append(new)
    stacked = [jnp.stack([n[i] for n in news], axis=0) for i in range(5)]
    return rms_norm(x, final_norm_g), stacked


def setup_inputs(seed: int = 0) -> dict:
    key = jax.random.key(seed)
    ks = iter(jax.random.split(key, 64))
    f32 = jnp.float32

    def nrm(shape, scale):
        return jax.random.normal(next(ks), shape, f32) * scale

    def gain(shape):
        return 1.0 + nrm(shape, 0.02)

    n_pages = PAST_LEN // PAGE_SIZE
    n_used = DEC_BATCH * n_pages
    n_pool = n_used + (n_used + 3) // 4
    perm = jax.random.permutation(next(ks), n_pool)
    page_table = perm[:n_used].reshape(DEC_BATCH, n_pages).astype(jnp.int32)
    return {
        'x_prompt': nrm((BATCH, SEQ, D_MODEL), 1.0),
        'x_sample': nrm((DEC_BATCH, DEC_SEQ, D_MODEL), 1.0),
        'cache_mla_kv': nrm((DEPTH, n_pool, PAGE_SIZE, KV_LORA + MLA_ROPE), 1.0),
        'cache_diff_k': nrm((DEPTH, n_pool, PAGE_SIZE, DIFF_KV, 2, DIFF_DH), 1.0),
        'cache_diff_v': nrm((DEPTH, n_pool, PAGE_SIZE, DIFF_KV, DIFF_VD), 1.0),
        'state_conv': nrm((DEPTH, DEC_BATCH, CONV_W - 1, CONV_CH), 0.5),
        'state_pool': nrm((DEPTH, DEC_BATCH, POOL_PREV, POOL_CH), 1.0),
        'page_table': page_table,
        'norm_mix_g': gain((DEPTH, D_MODEL)),
        'w_in': nrm((DEPTH, D_MODEL, IN_COLS), D_MODEL ** -0.5),
        'mla_q_norm_g': gain((DEPTH, Q_LORA)),
        'mla_w_uq': nrm((DEPTH, Q_LORA, MLA_H, MLA_NOPE + MLA_ROPE), Q_LORA ** -0.5),
        'mla_kv_norm_g': gain((DEPTH, KV_LORA)),
        'mla_w_uk': nrm((DEPTH, KV_LORA, MLA_H, MLA_NOPE), KV_LORA ** -0.5),
        'mla_w_uv': nrm((DEPTH, KV_LORA, MLA_H, MLA_VH), KV_LORA ** -0.5),
        'conv_w': nrm((DEPTH, CONV_W, CONV_CH), CONV_W ** -0.5),
        'conv_b': nrm((DEPTH, CONV_CH), 0.02),
        'conv_ln_g': gain((DEPTH, CONV_CH)),
        'conv_ln_b': nrm((DEPTH, CONV_CH), 0.02),
        'pool_w': nrm((DEPTH, len(POOL_WINDOWS), POOL_GROUP, POOL_GROUP), POOL_GROUP ** -0.5),
        'pool_scale': gain((DEPTH, POOL_CH)),
        'diff_lq1': nrm((DEPTH, DIFF_DH), 0.1),
        'diff_lk1': nrm((DEPTH, DIFF_DH), 0.1),
        'diff_lq2': nrm((DEPTH, DIFF_DH), 0.1),
        'diff_lk2': nrm((DEPTH, DIFF_DH), 0.1),
        'diff_subln_g': gain((DEPTH, DIFF_VD)),
        'w_branch': nrm((DEPTH, N_BRANCH, BRANCH, D_MODEL), BRANCH ** -0.5),
        'w_out': nrm((DEPTH, D_MODEL, D_MODEL), D_MODEL ** -0.5),
        'norm_ffn_g': gain((DEPTH, D_MODEL)),
        'router_g_w': nrm((DEPTH, D_MODEL, N_GROUPS), D_MODEL ** -0.5),
        'router_g_b': nrm((DEPTH, N_GROUPS), 0.01),
        'router_e_w': nrm((DEPTH, N_GROUPS, D_MODEL, E_PER_GROUP), D_MODEL ** -0.5),
        'router_e_b': nrm((DEPTH, N_GROUPS, E_PER_GROUP), 0.01),
        'moe_w_gate': nrm((DEPTH, N_EXPERTS, D_MODEL, EXPERT_FF), D_MODEL ** -0.5),
        'moe_w_up': nrm((DEPTH, N_EXPERTS, D_MODEL, EXPERT_FF), D_MODEL ** -0.5),
        'moe_w_down': nrm((DEPTH, N_EXPERTS, EXPERT_FF, D_MODEL), EXPERT_FF ** -0.5),
        'final_norm_g': gain((D_MODEL,)),
    }


def reference(x_prompt, x_sample, cache_mla_kv, cache_diff_k, cache_diff_v, state_conv, state_pool, page_table,
              norm_mix_g, w_in, mla_q_norm_g, mla_w_uq, mla_kv_norm_g, mla_w_uk, mla_w_uv,
              conv_w, conv_b, conv_ln_g, conv_ln_b, pool_w, pool_scale,
              diff_lq1, diff_lk1, diff_lq2, diff_lk2, diff_subln_g,
              w_branch, w_out, norm_ffn_g, router_g_w, router_g_b, router_e_w, router_e_b,
              moe_w_gate, moe_w_up, moe_w_down, final_norm_g):
    layers = [dict(norm_mix=norm_mix_g[l], w_in=w_in[l], mla_q_norm=mla_q_norm_g[l], mla_w_uq=mla_w_uq[l],
                   mla_kv_norm=mla_kv_norm_g[l], mla_w_uk=mla_w_uk[l], mla_w_uv=mla_w_uv[l],
                   conv_w=conv_w[l], conv_b=conv_b[l], conv_ln_g=conv_ln_g[l], conv_ln_b=conv_ln_b[l],
                   pool_w=pool_w[l], pool_scale=pool_scale[l],
                   lq1=diff_lq1[l], lk1=diff_lk1[l], lq2=diff_lq2[l], lk2=diff_lk2[l], diff_subln=diff_subln_g[l],
                   w_branch=w_branch[l], w_out=w_out[l], norm_ffn=norm_ffn_g[l],
                   router_g_w=router_g_w[l], router_g_b=router_g_b[l],
                   router_e_w=router_e_w[l], router_e_b=router_e_b[l],
                   w_gate=moe_w_gate[l], w_up=moe_w_up[l], w_down=moe_w_down[l])
              for l in range(DEPTH)]

    def prompt_attend(l):
        def attend(q, k, v, scale):
            return causal_block_attention(q, k, v, scale)
        return attend, attend

    def sample_attend(l):
        def attend_mla(q, k, v, scale):
            return paged_attention(q, k, v, (cache_mla_kv,), l, mla_rows_to_kv, page_table, scale)

        def attend_diff(q, k, v, scale):
            return paged_attention(q, k, v, (cache_diff_k, cache_diff_v), l, diff_rows_to_kv, page_table, scale)
        return attend_mla, attend_diff

    b_p, t_p = x_prompt.shape[0], x_prompt.shape[1]
    pos_p = jnp.arange(t_p, dtype=jnp.int32)
    conv0 = jnp.zeros((DEPTH, b_p, CONV_W - 1, CONV_CH), x_prompt.dtype)
    pool0 = jnp.zeros((DEPTH, b_p, POOL_PREV, POOL_CH), x_prompt.dtype)
    y_prompt, (mla_p, dk_p, dv_p, conv_p, pool_p) = run_trunk(
        x_prompt, pos_p, conv0, pool0, prompt_attend, layers, final_norm_g)

    past_len = page_table.shape[1] * PAGE_SIZE
    pos_s = past_len + jnp.arange(x_sample.shape[1], dtype=jnp.int32)
    y_sample, (mla_s, dk_s, dv_s, conv_s, pool_s) = run_trunk(
        x_sample, pos_s, state_conv, state_pool, sample_attend, layers, final_norm_g)

    return (y_prompt, y_sample, mla_p, dk_p, dv_p, conv_p, pool_p, mla_s, dk_s, dv_s, conv_s, pool_s)
```

```python
import functools
import math

import jax
import jax.numpy as jnp
from jax import lax
from jax.experimental import pallas as pl
from jax.experimental.pallas import tpu as pltpu

F32 = jnp.float32
BF16 = jnp.bfloat16

D_MODEL = 2048
BRANCH = 512
N_BRANCH = 4
MLA_NOPE = 64
MLA_ROPE = 32
MLA_VH = 64
MLA_H = 8
Q_LORA = 384
KV_LORA = 128
MLA_ROW = KV_LORA + MLA_ROPE
MLA_THETA = 10000.0
MLA_SCALE = 1.0 / math.sqrt(MLA_NOPE + MLA_ROPE)
CONV_W = 31
CONV_PREV = CONV_W - 1
POOL_WINDOWS = (2, 4, 8, 16)
POOL_GROUP = 128
POOL_PREV = 15
DIFF_DH = 64
DIFF_VD = 128
DIFF_KV = 2
DIFF_REP = 2
DIFF_SCALE = 1.0 / math.sqrt(DIFF_DH)
ROT_DIM = 16
ROPE_THETA = 500000.0
N_GROUPS = 4
E_PER_GROUP = 4
N_EXPERTS = 16
EXPERT_FF = 512
EPS = 1e-6
NEG_INF = -1e30
LANES = 128

MIX_QC = 0
MIX_KVC = 384
MIX_POOL = 512
MIX_GLU_A = 1024
MIX_GLU_G = 1536
MIX_DQ = 2048
MIX_DK = 2560
MIX_DV = 2816
MIX_KROPE = 3072
MIX_COLS = 3200
GATE_COLS = N_BRANCH * D_MODEL
ROUTER_LANE0 = N_GROUPS

VMEM_LIMIT = 56 * 1024 * 1024


def _params(*sem):
    return pltpu.CompilerParams(dimension_semantics=sem, vmem_limit_bytes=VMEM_LIMIT)


def _tile(n, cands):
    for c in cands:
        if n % c == 0:
            return c
    raise ValueError(f"no tile for {n} in {cands}")


def _rms(x, g, eps):
    return x * lax.rsqrt(jnp.mean(x * x, axis=-1, keepdims=True) + eps) * g


def _inproj_kernel(x_ref, g_ref, w_ref, z_ref, h_ref):
    @pl.when(pl.program_id(1) == 0)
    def _():
        h_ref[...] = _rms(x_ref[...], g_ref[...], EPS).astype(BF16)

    z_ref[...] = jnp.dot(h_ref[...], w_ref[...], preferred_element_type=F32)


def _inproj(x, g, w_mix):
    n = x.shape[0]
    tm = _tile(n, (512, 256))
    tn = 640
    return pl.pallas_call(
        _inproj_kernel,
        grid=(n // tm, MIX_COLS // tn),
        in_specs=[pl.BlockSpec((tm, D_MODEL), lambda i, j: (i, 0)),
                  pl.BlockSpec((1, D_MODEL), lambda i, j: (0, 0)),
                  pl.BlockSpec((D_MODEL, tn), lambda i, j: (0, j))],
        out_specs=[pl.BlockSpec((tm, tn), lambda i, j: (i, j)),
                   pl.BlockSpec((tm, D_MODEL), lambda i, j: (i, 0))],
        out_shape=[jax.ShapeDtypeStruct((n, MIX_COLS), F32),
                   jax.ShapeDtypeStruct((n, D_MODEL), BF16)],
        compiler_params=_params("parallel", "arbitrary"),
        name="inproj",
    )(x, g, w_mix)


def _gates_kernel(h_ref, w_ref, o_ref):
    o_ref[...] = jax.nn.sigmoid(jnp.dot(h_ref[...], w_ref[...], preferred_element_type=F32))


def _gates(h, w_gate):
    n = h.shape[0]
    tm = _tile(n, (1024, 512, 256))
    tn = 1024
    return pl.pallas_call(
        _gates_kernel,
        grid=(n // tm, GATE_COLS // tn),
        in_specs=[pl.BlockSpec((tm, D_MODEL), lambda i, j: (i, 0)),
                  pl.BlockSpec((D_MODEL, tn), lambda i, j: (0, j))],
        out_specs=pl.BlockSpec((tm, tn), lambda i, j: (i, j)),
        out_shape=jax.ShapeDtypeStruct((n, GATE_COLS), F32),
        compiler_params=_params("parallel", "arbitrary"),
        name="gates",
    )(h, w_gate)


def _prep_kernel(qc_ref, kvc_ref, kr_ref, dq_ref, dk_ref, gq_ref, gkv_ref, wuq_ref, wuk_ref,
                 c8_ref, s8_ref, c16_ref, s16_ref, dc_ref, dlo_ref, dhi_ref,
                 qmla_ref, rows_ref, dqr_ref, dkr_ref):
    qn = _rms(qc_ref[...], gq_ref[...], EPS).astype(BF16)
    q = jnp.dot(qn, wuq_ref[...], preferred_element_type=F32)
    nope = q[:, :MLA_H * MLA_NOPE].astype(BF16)
    r1 = q[:, 512:640]
    r2 = q[:, 640:768]
    c8, s8 = c8_ref[...], s8_ref[...]
    pe1 = r1 * c8 - r2 * s8
    pe2 = r1 * s8 + r2 * c8
    half = MLA_ROPE // 2
    for h in range(MLA_H):
        lat = jnp.dot(nope[:, h * MLA_NOPE:(h + 1) * MLA_NOPE], wuk_ref[h], preferred_element_type=F32)
        qh = jnp.concatenate([lat, pe1[:, h * half:(h + 1) * half], pe2[:, h * half:(h + 1) * half]], axis=-1)
        qmla_ref[h] = (qh * MLA_SCALE).astype(qmla_ref.dtype)
    ckv = _rms(kvc_ref[...], gkv_ref[...], EPS)
    kr = kr_ref[...]
    k1 = kr[:, :half]
    k2 = kr[:, half:MLA_ROPE]
    c16, s16 = c16_ref[...], s16_ref[...]
    rows_ref[...] = jnp.concatenate([ckv, k1 * c16 - k2 * s16, k1 * s16 + k2 * c16], axis=-1)
    dc, dlo, dhi = dc_ref[...], dlo_ref[...], dhi_ref[...]

    def rot(x):
        return x * dc + pltpu.roll(x, LANES - ROT_DIM // 2, 1) * dlo + pltpu.roll(x, ROT_DIM // 2, 1) * dhi

    for c in range(4):
        sl = slice(c * LANES, (c + 1) * LANES)
        dqr_ref[:, sl] = (rot(dq_ref[:, sl]) * DIFF_SCALE).astype(dqr_ref.dtype)
    for c in range(2):
        sl = slice(c * LANES, (c + 1) * LANES)
        dkr_ref[:, sl] = rot(dk_ref[:, sl])


def _prep(zmix, gq, gkv, wuq, wuk, tabs, q_dtype):
    n = zmix.shape[0]
    tab_rows = tabs[0].shape[0]
    tm = _tile(math.gcd(n, tab_rows), (512, 256, 128, 64))
    nt = tab_rows // tm
    row = lambda w, c: pl.BlockSpec((tm, w), lambda i: (i, c))
    tab = lambda w: pl.BlockSpec((tm, w), lambda i: (i % nt, 0))
    full = lambda shape: pl.BlockSpec(shape, lambda i: (0,) * len(shape))
    return pl.pallas_call(
        _prep_kernel,
        grid=(n // tm,),
        in_specs=[row(Q_LORA, MIX_QC // Q_LORA), row(KV_LORA, MIX_KVC // KV_LORA), row(LANES, MIX_KROPE // LANES),
                  row(512, MIX_DQ // 512), row(256, MIX_DK // 256),
                  full((1, Q_LORA)), full((1, KV_LORA)), full((Q_LORA, 768)), full((MLA_H, MLA_NOPE, KV_LORA)),
                  tab(LANES), tab(LANES), tab(16), tab(16), tab(LANES), tab(LANES), tab(LANES)],
        out_specs=[pl.BlockSpec((MLA_H, tm, MLA_ROW), lambda i: (0, i, 0)),
                   pl.BlockSpec((tm, MLA_ROW), lambda i: (i, 0)),
                   pl.BlockSpec((tm, 512), lambda i: (i, 0)),
                   pl.BlockSpec((tm, 256), lambda i: (i, 0))],
        out_shape=[jax.ShapeDtypeStruct((MLA_H, n, MLA_ROW), q_dtype),
                   jax.ShapeDtypeStruct((n, MLA_ROW), F32),
                   jax.ShapeDtypeStruct((n, 512), q_dtype),
                   jax.ShapeDtypeStruct((n, 256), F32)],
        compiler_params=_params("parallel"),
        name="prep",
    )(zmix, zmix, zmix, zmix, zmix, gq, gkv, wuq, wuk, *tabs)


def _rope_tables(pos):
    posf = pos.astype(F32)[:, None]
    inv_m = 1.0 / (MLA_THETA ** (jnp.arange(0, MLA_ROPE, 2, dtype=F32) / MLA_ROPE))
    ang_m = posf * inv_m[None, :]
    c16, s16 = jnp.cos(ang_m), jnp.sin(ang_m)
    c8, s8 = jnp.tile(c16, (1, MLA_H)), jnp.tile(s16, (1, MLA_H))
    inv_d = 1.0 / (ROPE_THETA ** (jnp.arange(0, ROT_DIM, 2, dtype=F32) / ROT_DIM))
    ang_d = posf * inv_d[None, :]
    cd, sd = jnp.cos(ang_d), jnp.sin(ang_d)
    t = pos.shape[0]
    hr = ROT_DIM // 2
    one = jnp.ones((t, DIFF_DH - ROT_DIM), F32)
    zero = jnp.zeros((t, DIFF_DH - ROT_DIM), F32)
    z8 = jnp.zeros((t, hr), F32)
    dc = jnp.tile(jnp.concatenate([cd, cd, one], axis=1), (1, 2))
    dlo = jnp.tile(jnp.concatenate([-sd, z8, zero], axis=1), (1, 2))
    dhi = jnp.tile(jnp.concatenate([z8, sd, zero], axis=1), (1, 2))
    return c8, s8, c16, s16, dc, dlo, dhi


def _online(s, v, m_ref, l_ref, acc_ref):
    m_prev = m_ref[...]
    m_new = jnp.maximum(m_prev, jnp.max(s, axis=-1, keepdims=True))
    corr = jnp.exp(m_prev - m_new)
    p = jnp.exp(s - m_new)
    l_ref[...] = l_ref[...] * corr + jnp.sum(p, axis=-1, keepdims=True)
    acc_ref[...] = acc_ref[...] * corr + jnp.dot(p.astype(BF16), v, preferred_element_type=F32)
    m_ref[...] = m_new


def _nt_dot(a, b):
    return lax.dot_general(a, b, (((1,), (1,)), ((), ())), preferred_element_type=F32)


def _mla_prompt_kernel(q_ref, k_ref, wuv_ref, o_ref, m_ref, l_ref, acc_ref, *, tq):
    qi, ki = pl.program_id(1), pl.program_id(2)
    rows = MLA_H * tq

    @pl.when(ki == 0)
    def _():
        m_ref[...] = jnp.full(m_ref.shape, NEG_INF, F32)
        l_ref[...] = jnp.zeros(l_ref.shape, F32)
        acc_ref[...] = jnp.zeros(acc_ref.shape, F32)

    def step(masked):
        q = q_ref[...].reshape(rows, MLA_ROW)
        k = k_ref[...].astype(BF16)
        s = _nt_dot(q, k)
        if masked:
            t = lax.broadcasted_iota(jnp.int32, s.shape, 0) & (tq - 1)
            c = lax.broadcasted_iota(jnp.int32, s.shape, 1)
            s = jnp.where(c <= t, s, NEG_INF)
        _online(s, k[:, :KV_LORA], m_ref, l_ref, acc_ref)

    @pl.when(ki < qi)
    def _():
        step(False)

    @pl.when(ki == qi)
    def _():
        step(True)
        o = (acc_ref[...] / l_ref[...]).astype(BF16)
        outs = [jnp.dot(o[h * tq:(h + 1) * tq], wuv_ref[h], preferred_element_type=F32) for h in range(MLA_H)]
        o_ref[...] = jnp.concatenate(outs, axis=-1)


def _mla_prompt(qmla, rows, wuv, b, t):
    tq = _tile(t, (256, 128))
    assert tq & (tq - 1) == 0
    nq = t // tq
    return pl.pallas_call(
        functools.partial(_mla_prompt_kernel, tq=tq),
        grid=(b, nq, nq),
        in_specs=[pl.BlockSpec((MLA_H, tq, MLA_ROW), lambda bi, qi, ki: (0, bi * nq + qi, 0)),
                  pl.BlockSpec((tq, MLA_ROW), lambda bi, qi, ki: (bi * nq + jnp.minimum(ki, qi), 0)),
                  pl.BlockSpec((MLA_H, KV_LORA, MLA_VH), lambda bi, qi, ki: (0, 0, 0))],
        out_specs=pl.BlockSpec((tq, BRANCH), lambda bi, qi, ki: (bi * nq + qi, 0)),
        out_shape=jax.ShapeDtypeStruct((b * t, BRANCH), F32),
        scratch_shapes=[pltpu.VMEM((MLA_H * tq, 1), F32), pltpu.VMEM((MLA_H * tq, 1), F32),
                        pltpu.VMEM((MLA_H * tq, KV_LORA), F32)],
        compiler_params=_params("parallel", "arbitrary", "arbitrary"),
        name="mla_prompt",
    )(qmla, rows, wuv)


def _lambda(lq1_ref, lk1_ref, lq2_ref, lk2_ref, lam_init):
    a = jnp.exp(jnp.sum(lq1_ref[...] * lk1_ref[...], axis=-1, keepdims=True))
    b = jnp.exp(jnp.sum(lq2_ref[...] * lk2_ref[...], axis=-1, keepdims=True))
    return a - b + lam_init


def _diff_prompt_kernel(q_ref, k_ref, v_ref, lq1_ref, lk1_ref, lq2_ref, lk2_ref, sub_ref, o_ref,
                        qs_ref, m_ref, l_ref, acc_ref, *, tq, lam_init):
    qi, ki = pl.program_id(1), pl.program_id(2)
    rr = DIFF_REP * tq

    @pl.when(ki == 0)
    def _():
        m_ref[...] = jnp.full(m_ref.shape, NEG_INF, F32)
        l_ref[...] = jnp.zeros(l_ref.shape, F32)
        acc_ref[...] = jnp.zeros(acc_ref.shape, F32)
        q = q_ref[...]
        for g in range(DIFF_KV):
            for m in range(2):
                c = g * 2 + m
                for r in range(DIFF_REP):
                    col = g * 256 + r * 128 + m * DIFF_DH
                    qs_ref[c * rr + r * tq:c * rr + (r + 1) * tq, :] = q[:, col:col + DIFF_DH]

    def step(masked):
        k = k_ref[...].astype(BF16)
        v = v_ref[...].astype(BF16)
        for c in range(4):
            g = c // 2
            rs = slice(c * rr, (c + 1) * rr)
            s = _nt_dot(qs_ref[rs, :], k[:, c * DIFF_DH:(c + 1) * DIFF_DH])
            if masked:
                t = lax.broadcasted_iota(jnp.int32, s.shape, 0) & (tq - 1)
                cc = lax.broadcasted_iota(jnp.int32, s.shape, 1)
                s = jnp.where(cc <= t, s, NEG_INF)
            _online(s, v[:, g * DIFF_VD:(g + 1) * DIFF_VD], m_ref.at[rs, :], l_ref.at[rs, :], acc_ref.at[rs, :])

    @pl.when(ki < qi)
    def _():
        step(False)

    @pl.when(ki == qi)
    def _():
        step(True)
        lam = _lambda(lq1_ref, lk1_ref, lq2_ref, lk2_ref, lam_init)
        o = acc_ref[...] / l_ref[...]
        for g in range(DIFF_KV):
            for r in range(DIFF_REP):
                o0 = o[(2 * g) * rr + r * tq:(2 * g) * rr + (r + 1) * tq]
                o1 = o[(2 * g + 1) * rr + r * tq:(2 * g + 1) * rr + (r + 1) * tq]
                d = _rms(o0 - lam * o1, sub_ref[...], 1e-5) * (1.0 - lam_init)
                col = (g * DIFF_REP + r) * DIFF_VD
                o_ref[:, col:col + DIFF_VD] = d


def _diff_prompt(dqr, dkr, zmix, lam_p, sub, b, t, lam_init):
    tq = _tile(t, (256, 128))
    assert tq & (tq - 1) == 0
    nq = t // tq
    vec = lambda w: pl.BlockSpec((1, w), lambda bi, qi, ki: (0, 0))
    nrow = 4 * DIFF_REP * tq
    return pl.pallas_call(
        functools.partial(_diff_prompt_kernel, tq=tq, lam_init=lam_init),
        grid=(b, nq, nq),
        in_specs=[pl.BlockSpec((tq, 512), lambda bi, qi, ki: (bi * nq + qi, 0)),
                  pl.BlockSpec((tq, 256), lambda bi, qi, ki: (bi * nq + jnp.minimum(ki, qi), 0)),
                  pl.BlockSpec((tq, 256), lambda bi, qi, ki: (bi * nq + jnp.minimum(ki, qi), MIX_DV // 256)),
                  vec(DIFF_DH), vec(DIFF_DH), vec(DIFF_DH), vec(DIFF_DH), vec(DIFF_VD)],
        out_specs=pl.BlockSpec((tq, BRANCH), lambda bi, qi, ki: (bi * nq + qi, 0)),
        out_shape=jax.ShapeDtypeStruct((b * t, BRANCH), F32),
        scratch_shapes=[pltpu.VMEM((nrow, DIFF_DH), BF16), pltpu.VMEM((nrow, 1), F32), pltpu.VMEM((nrow, 1), F32),
                        pltpu.VMEM((nrow, DIFF_VD), F32)],
        compiler_params=_params("parallel", "arbitrary", "arbitrary"),
        name="diff_prompt",
    )(dqr, dkr, zmix, *lam_p, sub)


def _causal_new(s, tn):
    t = lax.broadcasted_iota(jnp.int32, s.shape, 0) & (tn - 1)
    c = lax.broadcasted_iota(jnp.int32, s.shape, 1)
    return jnp.where(c <= t, s, NEG_INF)


def _pad_new(x):
    pad = jnp.zeros((LANES - x.shape[0], x.shape[1]), x.dtype)
    return jnp.concatenate([x, pad], axis=0).astype(BF16)


def _mla_paged_kernel(pt_ref, q_ref, knew_ref, wuv_ref, *rest, npg, tn):
    pages = rest[:npg]
    o_ref, m_ref, l_ref, acc_ref = rest[npg:]
    j = pl.program_id(1)

    @pl.when(j == 0)
    def _():
        m_ref[...] = jnp.full(m_ref.shape, NEG_INF, F32)
        l_ref[...] = jnp.zeros(l_ref.shape, F32)
        acc_ref[...] = jnp.zeros(acc_ref.shape, F32)

    q = q_ref[...].reshape(MLA_H * tn, MLA_ROW).astype(BF16)
    k = jnp.concatenate([p[...] for p in pages], axis=0).astype(BF16)
    _online(_nt_dot(q, k), k[:, :KV_LORA], m_ref, l_ref, acc_ref)

    @pl.when(j == pl.num_programs(1) - 1)
    def _():
        kn = _pad_new(knew_ref[...])
        _online(_causal_new(_nt_dot(q, kn), tn), kn[:, :KV_LORA], m_ref, l_ref, acc_ref)
        o = acc_ref[...] / l_ref[...]
        outs = [jnp.dot(o[h * tn:(h + 1) * tn].astype(BF16), wuv_ref[h], preferred_element_type=F32)
                for h in range(MLA_H)]
        o_ref[...] = jnp.concatenate(outs, axis=-1)


def _mla_paged(page_table, qmla, rows, wuv, cache, layer, tn):
    nb, n_pages = page_table.shape
    ps = cache.shape[2]
    npg = _tile(n_pages, (32, 16, 8, 4, 2, 1))
    assert tn & (tn - 1) == 0

    def page_spec(i):
        return pl.BlockSpec((None, None, ps, MLA_ROW),
                            lambda b, j, pt: (layer, pt[b * n_pages + j * npg + i], 0, 0))

    grid_spec = pltpu.PrefetchScalarGridSpec(
        num_scalar_prefetch=1,
        grid=(nb, n_pages // npg),
        in_specs=[pl.BlockSpec((MLA_H, tn, MLA_ROW), lambda b, j, pt: (0, b, 0)),
                  pl.BlockSpec((tn, MLA_ROW), lambda b, j, pt: (b, 0)),
                  pl.BlockSpec((MLA_H, KV_LORA, MLA_VH), lambda b, j, pt: (0, 0, 0))]
                 + [page_spec(i) for i in range(npg)],
        out_specs=pl.BlockSpec((tn, BRANCH), lambda b, j, pt: (b, 0)),
        scratch_shapes=[pltpu.VMEM((MLA_H * tn, 1), F32), pltpu.VMEM((MLA_H * tn, 1), F32),
                        pltpu.VMEM((MLA_H * tn, KV_LORA), F32)],
    )
    return pl.pallas_call(
        functools.partial(_mla_paged_kernel, npg=npg, tn=tn),
        grid_spec=grid_spec,
        out_shape=jax.ShapeDtypeStruct((nb * tn, BRANCH), F32),
        compiler_params=_params("parallel", "arbitrary"),
        name="mla_paged",
    )(page_table.reshape(-1), qmla, rows, wuv, *([cache] * npg))


def _diff_paged_kernel(pt_ref, q_ref, knew_ref, vnew_ref, lq1_ref, lk1_ref, lq2_ref, lk2_ref, sub_ref, *rest,
                       npg, tn, lam_init):
    kpages = rest[:npg]
    vpages = rest[npg:2 * npg]
    o_ref, qbd_ref, m_ref, l_ref, acc_ref = rest[2 * npg:]
    j = pl.program_id(1)
    rr = DIFF_REP * tn

    @pl.when(j == 0)
    def _():
        m_ref[...] = jnp.full(m_ref.shape, NEG_INF, F32)
        l_ref[...] = jnp.zeros(l_ref.shape, F32)
        acc_ref[...] = jnp.zeros(acc_ref.shape, F32)
        qbd_ref[...] = jnp.zeros(qbd_ref.shape, F32)
        q = q_ref[...]
        for g in range(DIFF_KV):
            for m in range(2):
                c = g * 2 + m
                for r in range(DIFF_REP):
                    col = g * 256 + r * 128 + m * DIFF_DH
                    qbd_ref[c * rr + r * tn:c * rr + (r + 1) * tn, c * DIFF_DH:(c + 1) * DIFF_DH] = q[:, col:col + DIFF_DH]

    qbd = qbd_ref[...].astype(BF16)
    k = jnp.concatenate([p[...] for p in kpages], axis=0).astype(BF16)
    v = jnp.concatenate([p[...] for p in vpages], axis=0).astype(BF16)
    _online(_nt_dot(qbd, k), v, m_ref, l_ref, acc_ref)

    @pl.when(j == pl.num_programs(1) - 1)
    def _():
        kn = _pad_new(knew_ref[...])
        vn = _pad_new(vnew_ref[...])
        _online(_causal_new(_nt_dot(qbd, kn), tn), vn, m_ref, l_ref, acc_ref)
        lam = _lambda(lq1_ref, lk1_ref, lq2_ref, lk2_ref, lam_init)
        o = acc_ref[...] / l_ref[...]
        for g in range(DIFF_KV):
            vs = slice(g * DIFF_VD, (g + 1) * DIFF_VD)
            for r in range(DIFF_REP):
                o0 = o[(2 * g) * rr + r * tn:(2 * g) * rr + (r + 1) * tn, vs]
                o1 = o[(2 * g + 1) * rr + r * tn:(2 * g + 1) * rr + (r + 1) * tn, vs]
                d = _rms(o0 - lam * o1, sub_ref[...], 1e-5) * (1.0 - lam_init)
                col = (g * DIFF_REP + r) * DIFF_VD
                o_ref[:, col:col + DIFF_VD] = d


def _diff_paged(page_table, dqr, dkr, zmix, lam_p, sub, cache_k, cache_v, layer, tn, lam_init):
    nb, n_pages = page_table.shape
    ps = cache_k.shape[2]
    npg = _tile(n_pages, (16, 8, 4, 2, 1))
    assert tn & (tn - 1) == 0

    def page_spec(i):
        return pl.BlockSpec((None, None, ps, 256), lambda b, j, pt: (layer, pt[b * n_pages + j * npg + i], 0, 0))

    vec = lambda w: pl.BlockSpec((1, w), lambda b, j, pt: (0, 0))
    nrow = 4 * DIFF_REP * tn
    grid_spec = pltpu.PrefetchScalarGridSpec(
        num_scalar_prefetch=1,
        grid=(nb, n_pages // npg),
        in_specs=[pl.BlockSpec((tn, 512), lambda b, j, pt: (b, 0)),
                  pl.BlockSpec((tn, 256), lambda b, j, pt: (b, 0)),
                  pl.BlockSpec((tn, 256), lambda b, j, pt: (b, MIX_DV // 256)),
                  vec(DIFF_DH), vec(DIFF_DH), vec(DIFF_DH), vec(DIFF_DH), vec(DIFF_VD)]
                 + [page_spec(i) for i in range(npg)] * 2,
        out_specs=pl.BlockSpec((tn, BRANCH), lambda b, j, pt: (b, 0)),
        scratch_shapes=[pltpu.VMEM((nrow, 256), F32), pltpu.VMEM((nrow, 1), F32), pltpu.VMEM((nrow, 1), F32),
                        pltpu.VMEM((nrow, 256), F32)],
    )
    return pl.pallas_call(
        functools.partial(_diff_paged_kernel, npg=npg, tn=tn, lam_init=lam_init),
        grid_spec=grid_spec,
        out_shape=jax.ShapeDtypeStruct((nb * tn, BRANCH), F32),
        compiler_params=_params("parallel", "arbitrary"),
        name="diff_paged",
    )(page_table.reshape(-1), dqr, dkr, zmix, *lam_p, sub, *([cache_k] * npg), *([cache_v] * npg))


CONV_HALO = 32
POOL_HALO = 16


def _seq_kernel(pool_ref, ga_ref, gg_ref, pc_ref, pp_ref, cw_ref, cb_ref, lg_ref, lb_ref, pw_ref, ps_ref, cnt_ref,
                bout_ref, cout_ref, nconv_ref, npool_ref, cwin_ref, pwin_ref, conv_ref, *, tc):
    i = pl.program_id(1)

    @pl.when(i == 0)
    def _():
        cwin_ref[0:CONV_HALO, :] = pc_ref[0]
        pwin_ref[0:POOL_HALO, :] = pp_ref[0]

    cwin_ref[CONV_HALO:CONV_HALO + tc, :] = ga_ref[...] * jax.nn.sigmoid(gg_ref[...])
    x = pool_ref[...]
    pwin_ref[POOL_HALO:POOL_HALO + tc, :] = x

    rc = min(tc, 64)
    off = CONV_HALO - CONV_PREV
    for r0 in range(0, tc, rc):
        for c0 in range(0, BRANCH, LANES):
            acc = jnp.broadcast_to(cb_ref[:, c0:c0 + LANES], (rc, LANES))
            for w in range(CONV_W):
                acc = acc + cwin_ref[off + r0 + w:off + r0 + w + rc, c0:c0 + LANES] * cw_ref[w:w + 1, c0:c0 + LANES]
            conv_ref[r0:r0 + rc, c0:c0 + LANES] = acc
    cv = conv_ref[...]
    mu = jnp.mean(cv, axis=-1, keepdims=True)
    var = jnp.mean(jnp.square(cv - mu), axis=-1, keepdims=True)
    y = (cv - mu) * lax.rsqrt(var + 1e-5) * lg_ref[...] + lb_ref[...]
    bout_ref[...] = y * jax.nn.sigmoid(y)

    for g, w in enumerate(POOL_WINDOWS):
        cs = slice(g * POOL_GROUP, (g + 1) * POOL_GROUP)
        s = pwin_ref[POOL_HALO:POOL_HALO + tc, cs]
        for jj in range(1, w):
            s = s + pwin_ref[POOL_HALO - jj:POOL_HALO - jj + tc, cs]
        pooled = (s / cnt_ref[:, cs] - x[:, cs]).astype(BF16)
        cout_ref[:, cs] = jnp.dot(pooled, pw_ref[g], preferred_element_type=F32) * ps_ref[:, cs]

    @pl.when(i == pl.num_programs(1) - 1)
    def _():
        nconv_ref[0] = cwin_ref[tc + off:tc + CONV_HALO, :]
        npool_ref[0] = pwin_ref[tc + 1:tc + POOL_HALO, :]

    carry_c = cwin_ref[tc:tc + CONV_HALO, :]
    carry_p = pwin_ref[tc:tc + POOL_HALO, :]
    cwin_ref[0:CONV_HALO, :] = carry_c
    pwin_ref[0:POOL_HALO, :] = carry_p


def _seq_mixers(zmix, prev_conv, prev_pool, cw, cb, lg, lb, pw, ps, cnt, b, t):
    tc = _tile(t, (128, 64, 32, 16, 8))
    nt = t // tc
    col = lambda c: pl.BlockSpec((tc, BRANCH), lambda bi, i: (bi * nt + i, c))
    full = lambda shape: pl.BlockSpec(shape, lambda bi, i: (0,) * len(shape))
    return pl.pallas_call(
        functools.partial(_seq_kernel, tc=tc),
        grid=(b, nt),
        in_specs=[col(MIX_POOL // BRANCH), col(MIX_GLU_A // BRANCH), col(MIX_GLU_G // BRANCH),
                  pl.BlockSpec((1, CONV_HALO, BRANCH), lambda bi, i: (bi, 0, 0)),
                  pl.BlockSpec((1, POOL_HALO, BRANCH), lambda bi, i: (bi, 0, 0)),
                  full((CONV_W, BRANCH)), full((1, BRANCH)), full((1, BRANCH)), full((1, BRANCH)),
                  full((len(POOL_WINDOWS), POOL_GROUP, POOL_GROUP)), full((1, BRANCH)),
                  pl.BlockSpec((tc, BRANCH), lambda bi, i: (i, 0))],
        out_specs=[pl.BlockSpec((tc, BRANCH), lambda bi, i: (bi * nt + i, 0)),
                   pl.BlockSpec((tc, BRANCH), lambda bi, i: (bi * nt + i, 0)),
                   pl.BlockSpec((1, CONV_PREV, BRANCH), lambda bi, i: (bi, 0, 0)),
                   pl.BlockSpec((1, POOL_PREV, BRANCH), lambda bi, i: (bi, 0, 0))],
        out_shape=[jax.ShapeDtypeStruct((b * t, BRANCH), F32), jax.ShapeDtypeStruct((b * t, BRANCH), F32),
                   jax.ShapeDtypeStruct((b, CONV_PREV, BRANCH), F32), jax.ShapeDtypeStruct((b, POOL_PREV, BRANCH), F32)],
        scratch_shapes=[pltpu.VMEM((CONV_HALO + tc, BRANCH), F32), pltpu.VMEM((POOL_HALO + tc, BRANCH), F32),
                        pltpu.VMEM((tc, BRANCH), F32)],
        compiler_params=_params("parallel", "arbitrary"),
        name="seq_mixers",
    )(zmix, zmix, zmix, prev_conv, prev_pool, cw, cb, lg, lb, pw, ps, cnt)


def _merge_kernel(g0, g1, g2, g3, a_ref, b_ref, c_ref, d_ref, wb_ref, o_ref):
    acc = None
    for n, (g, br) in enumerate(((g0, a_ref), (g1, b_ref), (g2, c_ref), (g3, d_ref))):
        up = jnp.dot(br[...].astype(BF16), wb_ref[n], preferred_element_type=F32)
        term = g[...] * up
        acc = term if acc is None else acc + term
    o_ref[...] = acc.astype(BF16)


def _merge(gates, branches, wb):
    n = gates.shape[0]
    tm = _tile(n, (512, 256))
    tn = 512
    nj = D_MODEL // tn
    gate = lambda k: pl.BlockSpec((tm, tn), lambda i, j: (i, k * nj + j))
    br = pl.BlockSpec((tm, BRANCH), lambda i, j: (i, 0))
    return pl.pallas_call(
        _merge_kernel,
        grid=(n // tm, nj),
        in_specs=[gate(0), gate(1), gate(2), gate(3), br, br, br, br,
                  pl.BlockSpec((N_BRANCH, BRANCH, tn), lambda i, j: (0, 0, j))],
        out_specs=pl.BlockSpec((tm, tn), lambda i, j: (i, j)),
        out_shape=jax.ShapeDtypeStruct((n, D_MODEL), BF16),
        compiler_params=_params("parallel", "arbitrary"),
        name="merge",
    )(gates, gates, gates, gates, *branches, wb)


def _split_bf16(x):
    hi = x.astype(BF16)
    return hi, (x - hi.astype(F32)).astype(BF16)


def _outproj_kernel(m_ref, x_ref, w_ref, g_ref, rwh_ref, rwl_ref, rb_ref, xo_ref, h_ref, comb_ref):
    xn = x_ref[...] + jnp.dot(m_ref[...], w_ref[...], preferred_element_type=F32)
    xo_ref[...] = xn
    t = _rms(xn, g_ref[...], EPS)
    h_ref[...] = t.astype(BF16)
    t_hi, t_lo = _split_bf16(t)
    logits = (jnp.dot(t_hi, rwh_ref[...], preferred_element_type=F32)
              + jnp.dot(t_lo, rwh_ref[...], preferred_element_type=F32)
              + jnp.dot(t_hi, rwl_ref[...], preferred_element_type=F32)) + rb_ref[...]
    lane = lax.broadcasted_iota(jnp.int32, logits.shape, 1).astype(F32)
    big = 1e9
    gl = jnp.where(lane < N_GROUPS, logits, -jnp.inf)
    gmax = jnp.max(gl, axis=-1, keepdims=True)
    gidx = jnp.min(jnp.where(gl == gmax, lane, big), axis=-1, keepdims=True)
    gw = 1.0 / jnp.sum(jnp.exp(gl - gmax), axis=-1, keepdims=True)
    lo = ROUTER_LANE0 + E_PER_GROUP * gidx
    el = jnp.where((lane >= lo) & (lane < lo + E_PER_GROUP), logits, -jnp.inf)
    v1 = jnp.max(el, axis=-1, keepdims=True)
    i1 = jnp.min(jnp.where(el == v1, lane, big), axis=-1, keepdims=True)
    el2 = jnp.where(lane == i1, -jnp.inf, el)
    v2 = jnp.max(el2, axis=-1, keepdims=True)
    i2 = jnp.min(jnp.where(el2 == v2, lane, big), axis=-1, keepdims=True)
    e2 = jnp.exp(v2 - v1)
    den = 1.0 + e2
    comb_ref[...] = jnp.where(lane == i1, (1.0 / den) * gw, 0.0) + jnp.where(lane == i2, (e2 / den) * gw, 0.0)


def _outproj(merged, x, w_out, g, rwh, rwl, rb):
    n = x.shape[0]
    tm = _tile(n, (256, 128))
    full = lambda shape: pl.BlockSpec(shape, lambda i: (0,) * len(shape))
    row = lambda w: pl.BlockSpec((tm, w), lambda i: (i, 0))
    return pl.pallas_call(
        _outproj_kernel,
        grid=(n // tm,),
        in_specs=[row(D_MODEL), row(D_MODEL), full((D_MODEL, D_MODEL)), full((1, D_MODEL)),
                  full((D_MODEL, LANES)), full((D_MODEL, LANES)), full((1, LANES))],
        out_specs=[row(D_MODEL), row(D_MODEL), row(LANES)],
        out_shape=[jax.ShapeDtypeStruct((n, D_MODEL), F32), jax.ShapeDtypeStruct((n, D_MODEL), BF16),
                   jax.ShapeDtypeStruct((n, LANES), F32)],
        compiler_params=_params("parallel"),
        name="outproj",
    )(merged, x, w_out, g, rwh, rwl, rb)


def _moe_kernel(h_ref, comb_ref, x_ref, wg_ref, wu_ref, wd_ref, fg_ref, o_ref, acc_ref, *, final):
    e = pl.program_id(1)

    @pl.when(e == 0)
    def _():
        acc_ref[...] = jnp.zeros(acc_ref.shape, F32)

    h = h_ref[...]
    gate = jnp.dot(h, wg_ref[0], preferred_element_type=F32)
    up = jnp.dot(h, wu_ref[0], preferred_element_type=F32)
    comb = comb_ref[...]
    lane = lax.broadcasted_iota(jnp.int32, comb.shape, 1)
    c = jnp.sum(jnp.where(lane == ROUTER_LANE0 + e, comb, 0.0), axis=-1, keepdims=True)
    hid = (gate * jax.nn.sigmoid(gate) * up * c).astype(BF16)
    acc_ref[...] += jnp.dot(hid, wd_ref[0], preferred_element_type=F32)

    @pl.when(e == pl.num_programs(1) - 1)
    def _():
        y = x_ref[...] + acc_ref[...]
        o_ref[...] = _rms(y, fg_ref[...], EPS) if final else y


def _moe(h2, comb, x, wg, wu, wd, fg, final):
    n = x.shape[0]
    tm = _tile(n, (512, 256))
    row = lambda w: pl.BlockSpec((tm, w), lambda i, e: (i, 0))
    return pl.pallas_call(
        functools.partial(_moe_kernel, final=final),
        grid=(n // tm, N_EXPERTS),
        in_specs=[row(D_MODEL), row(LANES), row(D_MODEL),
                  pl.BlockSpec((1, D_MODEL, EXPERT_FF), lambda i, e: (e, 0, 0)),
                  pl.BlockSpec((1, D_MODEL, EXPERT_FF), lambda i, e: (e, 0, 0)),
                  pl.BlockSpec((1, EXPERT_FF, D_MODEL), lambda i, e: (e, 0, 0)),
                  pl.BlockSpec((1, D_MODEL), lambda i, e: (0, 0))],
        out_specs=row(D_MODEL),
        out_shape=jax.ShapeDtypeStruct((n, D_MODEL), F32),
        scratch_shapes=[pltpu.VMEM((tm, D_MODEL), F32)],
        compiler_params=_params("parallel", "arbitrary"),
        name="moe",
    )(h2, comb, x, wg, wu, wd, fg)


def _layer_weights(l, norm_mix_g, w_in, mla_q_norm_g, mla_w_uq, mla_kv_norm_g, mla_w_uk, mla_w_uv,
                   conv_w, conv_b, conv_ln_g, conv_ln_b, pool_w, pool_scale,
                   diff_lq1, diff_lk1, diff_lq2, diff_lk2, diff_subln_g,
                   w_branch, w_out, norm_ffn_g, router_g_w, router_g_b, router_e_w, router_e_b,
                   moe_w_gate, moe_w_up, moe_w_down):
    wi = w_in[l]
    o = [0]
    for s in (Q_LORA, KV_LORA, MLA_ROPE, 2 * BRANCH, BRANCH, 512, 256, 256, GATE_COLS):
        o.append(o[-1] + s)
    sl = lambda k: wi[:, o[k]:o[k + 1]]
    glu = sl(3)
    w_mix = jnp.concatenate([sl(0), sl(1), sl(4), glu[:, :BRANCH], glu[:, BRANCH:], sl(5), sl(6), sl(7), sl(2),
                             jnp.zeros((D_MODEL, LANES - MLA_ROPE), F32)], axis=1).astype(BF16)
    uq = mla_w_uq[l]
    half = MLA_ROPE // 2
    wuq = jnp.concatenate([uq[:, :, :MLA_NOPE].reshape(Q_LORA, -1),
                           uq[:, :, MLA_NOPE:MLA_NOPE + half].reshape(Q_LORA, -1),
                           uq[:, :, MLA_NOPE + half:].reshape(Q_LORA, -1)], axis=1).astype(BF16)
    rw = jnp.concatenate([router_g_w[l], jnp.transpose(router_e_w[l], (1, 0, 2)).reshape(D_MODEL, N_EXPERTS),
                          jnp.zeros((D_MODEL, LANES - N_GROUPS - N_EXPERTS), F32)], axis=1)
    rb = jnp.concatenate([router_g_b[l], router_e_b[l].reshape(-1),
                          jnp.zeros((LANES - N_GROUPS - N_EXPERTS,), F32)])[None, :]
    rwh = rw.astype(BF16)
    rwl = (rw - rwh.astype(F32)).astype(BF16)
    return dict(
        norm_mix=norm_mix_g[l][None, :], w_mix=w_mix, w_gate=sl(8).astype(BF16),
        gq=mla_q_norm_g[l][None, :], gkv=mla_kv_norm_g[l][None, :], wuq=wuq,
        wuk=jnp.transpose(mla_w_uk[l], (1, 2, 0)).astype(BF16), wuv=jnp.transpose(mla_w_uv[l], (1, 0, 2)).astype(BF16),
        cw=conv_w[l], cb=conv_b[l][None, :], lg=conv_ln_g[l][None, :], lb=conv_ln_b[l][None, :],
        pw=pool_w[l].astype(BF16), ps=pool_scale[l][None, :],
        lam=(diff_lq1[l][None, :], diff_lk1[l][None, :], diff_lq2[l][None, :], diff_lk2[l][None, :]),
        sub=diff_subln_g[l][None, :],
        wb=w_branch[l].astype(BF16), w_out=w_out[l].astype(BF16), norm_ffn=norm_ffn_g[l][None, :],
        rwh=rwh, rwl=rwl, rb=rb,
        wg=moe_w_gate[l].astype(BF16), wu=moe_w_up[l].astype(BF16), wd=moe_w_down[l].astype(BF16))


def _pool_counts(pos):
    cols = [jnp.broadcast_to(jnp.minimum(w, pos + 1).astype(F32)[:, None], (pos.shape[0], POOL_GROUP))
            for w in POOL_WINDOWS]
    return jnp.concatenate(cols, axis=1)


def _tile_rows(tab, rows):
    return jnp.tile(tab, (rows // tab.shape[0], 1)) if tab.shape[0] < rows else tab


def _trunk(x, pos, b, t, prev_conv, prev_pool, attend, weights, final_g, depth):
    n = b * t
    tab_rows = t if t >= 64 else 64 * t
    tabs = tuple(_tile_rows(tb, tab_rows) for tb in _rope_tables(pos))
    cnt = _pool_counts(pos)
    news = []
    for l in range(depth):
        p = weights[l]
        lam_init = 0.8 - 0.6 * math.exp(-0.3 * l)
        zmix, h = _inproj(x, p['norm_mix'], p['w_mix'])
        gates = _gates(h, p['w_gate'])
        qmla, rows, dqr, dkr = _prep(zmix, p['gq'], p['gkv'], p['wuq'], p['wuk'], tabs, attend.q_dtype)
        a_out, d_out = attend(l, p, qmla, rows, dqr, dkr, zmix, lam_init)
        pc = jnp.pad(prev_conv[l], ((0, 0), (CONV_HALO - CONV_PREV, 0), (0, 0)))
        pp = jnp.pad(prev_pool[l], ((0, 0), (POOL_HALO - POOL_PREV, 0), (0, 0)))
        b_out, c_out, new_conv, new_pool = _seq_mixers(zmix, pc, pp, p['cw'], p['cb'], p['lg'], p['lb'],
                                                       p['pw'], p['ps'], cnt, b, t)
        merged = _merge(gates, (a_out, b_out, c_out, d_out), p['wb'])
        x, h2, comb = _outproj(merged, x, p['w_out'], p['norm_ffn'], p['rwh'], p['rwl'], p['rb'])
        x = _moe(h2, comb, x, p['wg'], p['wu'], p['wd'], final_g, final=(l == depth - 1))
        news.append((rows.reshape(b, t, MLA_ROW), dkr.reshape(b, t, DIFF_KV, 2, DIFF_DH),
                     zmix[:, MIX_DV:MIX_DV + 256].reshape(b, t, DIFF_KV, DIFF_VD), new_conv, new_pool))
    stacked = [jnp.stack([nw[i] for nw in news], axis=0) for i in range(5)]
    return x.reshape(b, t, D_MODEL), stacked


class _PromptAttend:
    q_dtype = BF16

    def __init__(self, b, t):
        self.b, self.t = b, t

    def __call__(self, l, p, qmla, rows, dqr, dkr, zmix, lam_init):
        a_out = _mla_prompt(qmla, rows, p['wuv'], self.b, self.t)
        d_out = _diff_prompt(dqr, dkr, zmix, p['lam'], p['sub'], self.b, self.t, lam_init)
        return a_out, d_out


class _PagedAttend:
    q_dtype = F32

    def __init__(self, page_table, cache_mla, cache_k, cache_v, tn):
        self.pt, self.cm, self.tn = page_table, cache_mla, tn
        self.ck = cache_k.reshape(cache_k.shape[:3] + (256,))
        self.cv = cache_v.reshape(cache_v.shape[:3] + (256,))

    def __call__(self, l, p, qmla, rows, dqr, dkr, zmix, lam_init):
        a_out = _mla_paged(self.pt, qmla, rows, p['wuv'], self.cm, l, self.tn)
        d_out = _diff_paged(self.pt, dqr, dkr, zmix, p['lam'], p['sub'], self.ck, self.cv, l, self.tn, lam_init)
        return a_out, d_out


def kernel(x_prompt, x_sample, cache_mla_kv, cache_diff_k, cache_diff_v, state_conv, state_pool, page_table, norm_mix_g, w_in, mla_q_norm_g, mla_w_uq, mla_kv_norm_g, mla_w_uk, mla_w_uv, conv_w, conv_b, conv_ln_g, conv_ln_b, pool_w, pool_scale, diff_lq1, diff_lk1, diff_lq2, diff_lk2, diff_subln_g, w_branch, w_out, norm_ffn_g, router_g_w, router_g_b, router_e_w, router_e_b, moe_w_gate, moe_w_up, moe_w_down, final_norm_g):
    depth = w_in.shape[0]
    weights = [_layer_weights(l, norm_mix_g, w_in, mla_q_norm_g, mla_w_uq, mla_kv_norm_g, mla_w_uk, mla_w_uv,
                              conv_w, conv_b, conv_ln_g, conv_ln_b, pool_w, pool_scale,
                              diff_lq1, diff_lk1, diff_lq2, diff_lk2, diff_subln_g,
                              w_branch, w_out, norm_ffn_g, router_g_w, router_g_b, router_e_w, router_e_b,
                              moe_w_gate, moe_w_up, moe_w_down) for l in range(depth)]
    final_g = final_norm_g[None, :]

    bp, tp, _ = x_prompt.shape
    pos_p = jnp.arange(tp, dtype=jnp.int32)
    conv0 = jnp.zeros((depth, bp, CONV_PREV, BRANCH), F32)
    pool0 = jnp.zeros((depth, bp, POOL_PREV, BRANCH), F32)
    y_p, news_p = _trunk(x_prompt.reshape(bp * tp, D_MODEL), pos_p, bp, tp, conv0, pool0,
                         _PromptAttend(bp, tp), weights, final_g, depth)

    bs, ts, _ = x_sample.shape
    past_len = page_table.shape[1] * cache_mla_kv.shape[2]
    pos_s = past_len + jnp.arange(ts, dtype=jnp.int32)
    y_s, news_s = _trunk(x_sample.reshape(bs * ts, D_MODEL), pos_s, bs, ts, state_conv, state_pool,
                         _PagedAttend(page_table, cache_mla_kv, cache_diff_k, cache_diff_v, ts), weights, final_g, depth)

    return (y_p, y_s, *news_p, *news_s)
```

```python
import functools
import math

import jax
import jax.numpy as jnp
from jax import lax
from jax.experimental import pallas as pl
from jax.experimental.pallas import tpu as pltpu

F32 = jnp.float32
BF16 = jnp.bfloat16

D_MODEL = 2048
BRANCH = 512
N_BRANCH = 4
MLA_NOPE = 64
MLA_ROPE = 32
MLA_VH = 64
MLA_H = 8
Q_LORA = 384
KV_LORA = 128
MLA_ROW = KV_LORA + MLA_ROPE
MLA_THETA = 10000.0
LOG2E = math.log2(math.e)
MLA_SCALE = LOG2E / math.sqrt(MLA_NOPE + MLA_ROPE)
CONV_W = 31
CONV_PREV = CONV_W - 1
POOL_WINDOWS = (2, 4, 8, 16)
POOL_GROUP = 128
POOL_PREV = 15
DIFF_DH = 64
DIFF_VD = 128
DIFF_KV = 2
DIFF_REP = 2
DIFF_SCALE = LOG2E / math.sqrt(DIFF_DH)
ROT_DIM = 16
ROPE_THETA = 500000.0
N_GROUPS = 4
E_PER_GROUP = 4
N_EXPERTS = 16
EXPERT_FF = 512
EPS = 1e-6
NEG_INF = -1e30
LANES = 128

MIX_QC = 0
MIX_KVC = 384
MIX_POOL = 512
MIX_GLU_A = 1024
MIX_GLU_G = 1536
MIX_DQ = 2048
MIX_DK = 2560
MIX_DV = 2816
MIX_KROPE = 3072
MIX_COLS = 3200
GATE_COLS = N_BRANCH * D_MODEL
ROUTER_LANE0 = N_GROUPS

VMEM_LIMIT = 56 * 1024 * 1024


def _params(*sem):
    return pltpu.CompilerParams(dimension_semantics=sem, vmem_limit_bytes=VMEM_LIMIT)


def _tile(n, cands):
    for c in cands:
        if n % c == 0:
            return c
    raise ValueError(f"no tile for {n} in {cands}")


def _rms(x, g, eps):
    return x * lax.rsqrt(jnp.mean(x * x, axis=-1, keepdims=True) + eps) * g


def _inproj_kernel(x_ref, g_ref, w_ref, z_ref, h_ref):
    @pl.when(pl.program_id(1) == 0)
    def _():
        h_ref[...] = _rms(x_ref[...], g_ref[...], EPS).astype(BF16)

    z_ref[...] = jnp.dot(h_ref[...], w_ref[...], preferred_element_type=F32)


def _inproj(x, g, w_mix):
    n = x.shape[0]
    tm = _tile(n, (512, 256))
    tn = 640
    return pl.pallas_call(
        _inproj_kernel,
        grid=(n // tm, MIX_COLS // tn),
        in_specs=[pl.BlockSpec((tm, D_MODEL), lambda i, j: (i, 0)),
                  pl.BlockSpec((1, D_MODEL), lambda i, j: (0, 0)),
                  pl.BlockSpec((D_MODEL, tn), lambda i, j: (0, j))],
        out_specs=[pl.BlockSpec((tm, tn), lambda i, j: (i, j)),
                   pl.BlockSpec((tm, D_MODEL), lambda i, j: (i, 0))],
        out_shape=[jax.ShapeDtypeStruct((n, MIX_COLS), F32),
                   jax.ShapeDtypeStruct((n, D_MODEL), BF16)],
        compiler_params=_params("parallel", "arbitrary"),
        name="inproj",
    )(x, g, w_mix)


def _gates_kernel(h_ref, w_ref, o_ref):
    o_ref[...] = jax.nn.sigmoid(jnp.dot(h_ref[...], w_ref[...], preferred_element_type=F32))


def _gates(h, w_gate):
    n = h.shape[0]
    tm = _tile(n, (1024, 512, 256))
    tn = 1024
    return pl.pallas_call(
        _gates_kernel,
        grid=(n // tm, GATE_COLS // tn),
        in_specs=[pl.BlockSpec((tm, D_MODEL), lambda i, j: (i, 0)),
                  pl.BlockSpec((D_MODEL, tn), lambda i, j: (0, j))],
        out_specs=pl.BlockSpec((tm, tn), lambda i, j: (i, j)),
        out_shape=jax.ShapeDtypeStruct((n, GATE_COLS), F32),
        compiler_params=_params("parallel", "arbitrary"),
        name="gates",
    )(h, w_gate)


def _prep_kernel(qc_ref, kvc_ref, kr_ref, dq_ref, dk_ref, gq_ref, gkv_ref, wuq_ref, wuk_ref,
                 c8_ref, s8_ref, c16_ref, s16_ref, dc_ref, dlo_ref, dhi_ref,
                 qmla_ref, rows_ref, dqr_ref, dkr_ref):
    qn = _rms(qc_ref[...], gq_ref[...], EPS).astype(BF16)
    q = jnp.dot(qn, wuq_ref[...], preferred_element_type=F32)
    nope = q[:, :MLA_H * MLA_NOPE].astype(BF16)
    r1 = q[:, 512:640]
    r2 = q[:, 640:768]
    c8, s8 = c8_ref[...], s8_ref[...]
    pe1 = r1 * c8 - r2 * s8
    pe2 = r1 * s8 + r2 * c8
    half = MLA_ROPE // 2
    for h in range(MLA_H):
        lat = jnp.dot(nope[:, h * MLA_NOPE:(h + 1) * MLA_NOPE], wuk_ref[h], preferred_element_type=F32)
        qh = jnp.concatenate([lat, pe1[:, h * half:(h + 1) * half], pe2[:, h * half:(h + 1) * half]], axis=-1)
        qmla_ref[h] = (qh * MLA_SCALE).astype(qmla_ref.dtype)
    ckv = _rms(kvc_ref[...], gkv_ref[...], EPS)
    kr = kr_ref[...]
    k1 = kr[:, :half]
    k2 = kr[:, half:MLA_ROPE]
    c16, s16 = c16_ref[...], s16_ref[...]
    rows_ref[...] = jnp.concatenate([ckv, k1 * c16 - k2 * s16, k1 * s16 + k2 * c16], axis=-1)
    dc, dlo, dhi = dc_ref[...], dlo_ref[...], dhi_ref[...]

    def rot(x):
        return x * dc + pltpu.roll(x, LANES - ROT_DIM // 2, 1) * dlo + pltpu.roll(x, ROT_DIM // 2, 1) * dhi

    for c in range(4):
        sl = slice(c * LANES, (c + 1) * LANES)
        dqr_ref[:, sl] = (rot(dq_ref[:, sl]) * DIFF_SCALE).astype(dqr_ref.dtype)
    for c in range(2):
        sl = slice(c * LANES, (c + 1) * LANES)
        dkr_ref[:, sl] = rot(dk_ref[:, sl])


def _prep(zmix, gq, gkv, wuq, wuk, tabs, q_dtype):
    n = zmix.shape[0]
    tab_rows = tabs[0].shape[0]
    tm = _tile(math.gcd(n, tab_rows), (512, 256, 128, 64))
    nt = tab_rows // tm
    row = lambda w, c: pl.BlockSpec((tm, w), lambda i: (i, c))
    tab = lambda w: pl.BlockSpec((tm, w), lambda i: (i % nt, 0))
    full = lambda shape: pl.BlockSpec(shape, lambda i: (0,) * len(shape))
    return pl.pallas_call(
        _prep_kernel,
        grid=(n // tm,),
        in_specs=[row(Q_LORA, MIX_QC // Q_LORA), row(KV_LORA, MIX_KVC // KV_LORA), row(LANES, MIX_KROPE // LANES),
                  row(512, MIX_DQ // 512), row(256, MIX_DK // 256),
                  full((1, Q_LORA)), full((1, KV_LORA)), full((Q_LORA, 768)), full((MLA_H, MLA_NOPE, KV_LORA)),
                  tab(LANES), tab(LANES), tab(16), tab(16), tab(LANES), tab(LANES), tab(LANES)],
        out_specs=[pl.BlockSpec((MLA_H, tm, MLA_ROW), lambda i: (0, i, 0)),
                   pl.BlockSpec((tm, MLA_ROW), lambda i: (i, 0)),
                   pl.BlockSpec((tm, 512), lambda i: (i, 0)),
                   pl.BlockSpec((tm, 256), lambda i: (i, 0))],
        out_shape=[jax.ShapeDtypeStruct((MLA_H, n, MLA_ROW), q_dtype),
                   jax.ShapeDtypeStruct((n, MLA_ROW), F32),
                   jax.ShapeDtypeStruct((n, 512), q_dtype),
                   jax.ShapeDtypeStruct((n, 256), F32)],
        compiler_params=_params("parallel"),
        name="prep",
    )(zmix, zmix, zmix, zmix, zmix, gq, gkv, wuq, wuk, *tabs)


def _rope_tables(pos):
    posf = pos.astype(F32)[:, None]
    inv_m = 1.0 / (MLA_THETA ** (jnp.arange(0, MLA_ROPE, 2, dtype=F32) / MLA_ROPE))
    ang_m = posf * inv_m[None, :]
    c16, s16 = jnp.cos(ang_m), jnp.sin(ang_m)
    c8, s8 = jnp.tile(c16, (1, MLA_H)), jnp.tile(s16, (1, MLA_H))
    inv_d = 1.0 / (ROPE_THETA ** (jnp.arange(0, ROT_DIM, 2, dtype=F32) / ROT_DIM))
    ang_d = posf * inv_d[None, :]
    cd, sd = jnp.cos(ang_d), jnp.sin(ang_d)
    t = pos.shape[0]
    hr = ROT_DIM // 2
    one = jnp.ones((t, DIFF_DH - ROT_DIM), F32)
    zero = jnp.zeros((t, DIFF_DH - ROT_DIM), F32)
    z8 = jnp.zeros((t, hr), F32)
    dc = jnp.tile(jnp.concatenate([cd, cd, one], axis=1), (1, 2))
    dlo = jnp.tile(jnp.concatenate([-sd, z8, zero], axis=1), (1, 2))
    dhi = jnp.tile(jnp.concatenate([z8, sd, zero], axis=1), (1, 2))
    return c8, s8, c16, s16, dc, dlo, dhi


def _nt_dot(a, b):
    return lax.dot_general(a, b, (((1,), (1,)), ((), ())), preferred_element_type=F32)


def _init_state(m_ref, l_ref, acc_ref):
    m_ref[...] = jnp.full(m_ref.shape, NEG_INF, F32)
    l_ref[...] = jnp.zeros(l_ref.shape, F32)
    acc_ref[...] = jnp.zeros(acc_ref.shape, F32)


def _probs(s, m_ref, l_ref):
    m_prev = m_ref[...]
    m_new = jnp.maximum(m_prev, jnp.max(s, axis=1, keepdims=True))
    corr = jnp.exp2(m_prev - m_new)
    ps = [jnp.exp2(s[:, c:c + LANES] - m_new) for c in range(0, s.shape[1], LANES)]
    lsum = ps[0]
    for p in ps[1:]:
        lsum = lsum + p
    l_ref[...] = l_ref[...] * corr + lsum
    m_ref[...] = m_new
    p = ps[0] if len(ps) == 1 else jnp.concatenate(ps, axis=1)
    return p.astype(BF16), corr


def _causal(s, tq):
    t = lax.broadcasted_iota(jnp.int32, s.shape, 0) & (tq - 1)
    c = lax.broadcasted_iota(jnp.int32, s.shape, 1)
    return jnp.where(c <= t, s, NEG_INF)


def _tri_pairs(nq):
    qi = [q for q in range(nq) for _ in range(q + 1)]
    ki = [k for q in range(nq) for k in range(q + 1)]
    return jnp.asarray(qi, jnp.int32), jnp.asarray(ki, jnp.int32)


def _mla_prompt_kernel(qi_ref, ki_ref, q_ref, k_ref, wuv_ref, o_ref, m_ref, l_ref, acc_ref, *, tq):
    pair = pl.program_id(1)
    qi, ki = qi_ref[pair], ki_ref[pair]

    @pl.when(ki == 0)
    def _():
        _init_state(m_ref, l_ref, acc_ref)

    def step(masked):
        k = k_ref[...].astype(BF16)
        v = k[:, :KV_LORA]
        for h in range(MLA_H):
            rs = slice(h * tq, (h + 1) * tq)
            s = _nt_dot(q_ref[h], k)
            if masked:
                s = _causal(s, tq)
            p, corr = _probs(s, m_ref.at[rs, :], l_ref.at[rs, :])
            acc_ref[rs, :] = acc_ref[rs, :] * corr + jnp.dot(p, v, preferred_element_type=F32)

    @pl.when(ki < qi)
    def _():
        step(False)

    @pl.when(ki == qi)
    def _():
        step(True)
        outs = []
        for h in range(MLA_H):
            rs = slice(h * tq, (h + 1) * tq)
            o = acc_ref[rs, :] / jnp.sum(l_ref[rs, :], axis=1, keepdims=True)
            outs.append(jnp.dot(o.astype(BF16), wuv_ref[h], preferred_element_type=F32))
        o_ref[...] = jnp.concatenate(outs, axis=-1)


def _mla_prompt(qmla, rows, wuv, b, t):
    tq = _tile(t, (256, 128))
    assert tq & (tq - 1) == 0
    nq = t // tq
    qi_tab, ki_tab = _tri_pairs(nq)
    grid_spec = pltpu.PrefetchScalarGridSpec(
        num_scalar_prefetch=2,
        grid=(b, qi_tab.shape[0]),
        in_specs=[pl.BlockSpec((MLA_H, tq, MLA_ROW), lambda bi, p, qt, kt: (0, bi * nq + qt[p], 0)),
                  pl.BlockSpec((tq, MLA_ROW), lambda bi, p, qt, kt: (bi * nq + kt[p], 0)),
                  pl.BlockSpec((MLA_H, KV_LORA, MLA_VH), lambda bi, p, qt, kt: (0, 0, 0))],
        out_specs=pl.BlockSpec((tq, BRANCH), lambda bi, p, qt, kt: (bi * nq + qt[p], 0)),
        scratch_shapes=[pltpu.VMEM((MLA_H * tq, LANES), F32), pltpu.VMEM((MLA_H * tq, LANES), F32),
                        pltpu.VMEM((MLA_H * tq, KV_LORA), F32)],
    )
    return pl.pallas_call(
        functools.partial(_mla_prompt_kernel, tq=tq),
        grid_spec=grid_spec,
        out_shape=jax.ShapeDtypeStruct((b * t, BRANCH), F32),
        compiler_params=_params("parallel", "arbitrary"),
        name="mla_prompt",
    )(qi_tab, ki_tab, qmla, rows, wuv)


def _lambda(lq1_ref, lk1_ref, lq2_ref, lk2_ref, lam_init):
    a = jnp.exp(jnp.sum(lq1_ref[...] * lk1_ref[...], axis=-1, keepdims=True))
    b = jnp.exp(jnp.sum(lq2_ref[...] * lk2_ref[...], axis=-1, keepdims=True))
    return a - b + lam_init


def _diff_prompt_kernel(qi_ref, ki_ref, q_ref, k_ref, v_ref, lq1_ref, lk1_ref, lq2_ref, lk2_ref, sub_ref, o_ref,
                        qs_ref, m_ref, l_ref, acc_ref, *, tq, lam_init):
    pair = pl.program_id(1)
    qi, ki = qi_ref[pair], ki_ref[pair]
    rr = DIFF_REP * tq

    @pl.when(ki == 0)
    def _():
        _init_state(m_ref, l_ref, acc_ref)
        q = q_ref[...]
        for g in range(DIFF_KV):
            for m in range(2):
                c = g * 2 + m
                for r in range(DIFF_REP):
                    col = g * 256 + r * 128 + m * DIFF_DH
                    qs_ref[c * rr + r * tq:c * rr + (r + 1) * tq, :] = q[:, col:col + DIFF_DH]

    def step(masked):
        k = k_ref[...].astype(BF16)
        v = v_ref[...].astype(BF16)
        for c in range(4):
            g = c // 2
            rs = slice(c * rr, (c + 1) * rr)
            s = _nt_dot(qs_ref[rs, :], k[:, c * DIFF_DH:(c + 1) * DIFF_DH])
            if masked:
                s = _causal(s, tq)
            p, corr = _probs(s, m_ref.at[rs, :], l_ref.at[rs, :])
            acc_ref[rs, :] = acc_ref[rs, :] * corr + jnp.dot(p, v[:, g * DIFF_VD:(g + 1) * DIFF_VD],
                                                             preferred_element_type=F32)

    @pl.when(ki < qi)
    def _():
        step(False)

    @pl.when(ki == qi)
    def _():
        step(True)
        lam = _lambda(lq1_ref, lk1_ref, lq2_ref, lk2_ref, lam_init)
        o = acc_ref[...] / jnp.sum(l_ref[...], axis=1, keepdims=True)
        for g in range(DIFF_KV):
            for r in range(DIFF_REP):
                o0 = o[(2 * g) * rr + r * tq:(2 * g) * rr + (r + 1) * tq]
                o1 = o[(2 * g + 1) * rr + r * tq:(2 * g + 1) * rr + (r + 1) * tq]
                d = _rms(o0 - lam * o1, sub_ref[...], 1e-5) * (1.0 - lam_init)
                col = (g * DIFF_REP + r) * DIFF_VD
                o_ref[:, col:col + DIFF_VD] = d


def _diff_prompt(dqr, dkr, zmix, lam_p, sub, b, t, lam_init):
    tq = _tile(t, (256, 128))
    assert tq & (tq - 1) == 0
    nq = t // tq
    qi_tab, ki_tab = _tri_pairs(nq)
    vec = lambda w: pl.BlockSpec((1, w), lambda bi, p, qt, kt: (0, 0))
    nrow = 4 * DIFF_REP * tq
    grid_spec = pltpu.PrefetchScalarGridSpec(
        num_scalar_prefetch=2,
        grid=(b, qi_tab.shape[0]),
        in_specs=[pl.BlockSpec((tq, 512), lambda bi, p, qt, kt: (bi * nq + qt[p], 0)),
                  pl.BlockSpec((tq, 256), lambda bi, p, qt, kt: (bi * nq + kt[p], 0)),
                  pl.BlockSpec((tq, 256), lambda bi, p, qt, kt: (bi * nq + kt[p], MIX_DV // 256)),
                  vec(DIFF_DH), vec(DIFF_DH), vec(DIFF_DH), vec(DIFF_DH), vec(DIFF_VD)],
        out_specs=pl.BlockSpec((tq, BRANCH), lambda bi, p, qt, kt: (bi * nq + qt[p], 0)),
        scratch_shapes=[pltpu.VMEM((nrow, DIFF_DH), BF16), pltpu.VMEM((nrow, LANES), F32),
                        pltpu.VMEM((nrow, LANES), F32), pltpu.VMEM((nrow, DIFF_VD), F32)],
    )
    return pl.pallas_call(
        functools.partial(_diff_prompt_kernel, tq=tq, lam_init=lam_init),
        grid_spec=grid_spec,
        out_shape=jax.ShapeDtypeStruct((b * t, BRANCH), F32),
        compiler_params=_params("parallel", "arbitrary"),
        name="diff_prompt",
    )(qi_tab, ki_tab, dqr, dkr, zmix, *lam_p, sub)


def _pad_new(x):
    pad = jnp.zeros((LANES - x.shape[0], x.shape[1]), x.dtype)
    return jnp.concatenate([x, pad], axis=0).astype(BF16)


def _mla_paged_kernel(pt_ref, q_ref, knew_ref, wuv_ref, *rest, npg, tn):
    pages = rest[:npg]
    o_ref, m_ref, l_ref, acc_ref = rest[npg:]
    j = pl.program_id(1)

    @pl.when(j == 0)
    def _():
        _init_state(m_ref, l_ref, acc_ref)

    q = q_ref[...].reshape(MLA_H * tn, MLA_ROW).astype(BF16)
    kt = jnp.concatenate([p[...] for p in pages], axis=1).astype(BF16)
    p, corr = _probs(jnp.dot(q, kt, preferred_element_type=F32), m_ref, l_ref)
    acc_ref[...] = acc_ref[...] * corr + _nt_dot(p, kt[:KV_LORA, :])

    @pl.when(j == pl.num_programs(1) - 1)
    def _():
        kn = _pad_new(knew_ref[...])
        pn, cn = _probs(_causal(_nt_dot(q, kn), tn), m_ref, l_ref)
        acc = acc_ref[...] * cn + jnp.dot(pn, kn[:, :KV_LORA], preferred_element_type=F32)
        o = acc / jnp.sum(l_ref[...], axis=1, keepdims=True)
        outs = [jnp.dot(o[h * tn:(h + 1) * tn].astype(BF16), wuv_ref[h], preferred_element_type=F32)
                for h in range(MLA_H)]
        o_ref[...] = jnp.concatenate(outs, axis=-1)


def _mla_paged(page_table, qmla, rows, wuv, cache_t, layer, tn):
    nb, n_pages = page_table.shape
    ps = cache_t.shape[3]
    npg = _tile(n_pages, (32, 16, 8, 4, 2, 1))
    assert tn & (tn - 1) == 0

    def page_spec(i):
        return pl.BlockSpec((None, None, MLA_ROW, ps),
                            lambda b, j, pt: (layer, pt[b * n_pages + j * npg + i], 0, 0))

    grid_spec = pltpu.PrefetchScalarGridSpec(
        num_scalar_prefetch=1,
        grid=(nb, n_pages // npg),
        in_specs=[pl.BlockSpec((MLA_H, tn, MLA_ROW), lambda b, j, pt: (0, b, 0)),
                  pl.BlockSpec((tn, MLA_ROW), lambda b, j, pt: (b, 0)),
                  pl.BlockSpec((MLA_H, KV_LORA, MLA_VH), lambda b, j, pt: (0, 0, 0))]
                 + [page_spec(i) for i in range(npg)],
        out_specs=pl.BlockSpec((tn, BRANCH), lambda b, j, pt: (b, 0)),
        scratch_shapes=[pltpu.VMEM((MLA_H * tn, LANES), F32), pltpu.VMEM((MLA_H * tn, LANES), F32),
                        pltpu.VMEM((MLA_H * tn, KV_LORA), F32)],
    )
    return pl.pallas_call(
        functools.partial(_mla_paged_kernel, npg=npg, tn=tn),
        grid_spec=grid_spec,
        out_shape=jax.ShapeDtypeStruct((nb * tn, BRANCH), F32),
        compiler_params=_params("parallel", "arbitrary"),
        name="mla_paged",
    )(page_table.reshape(-1), qmla, rows, wuv, *([cache_t] * npg))


def _diff_paged_kernel(pt_ref, q_ref, knew_ref, vnew_ref, lq1_ref, lk1_ref, lq2_ref, lk2_ref, sub_ref, *rest,
                       npg, tn, ps, lam_init):
    kpages = rest[:npg]
    vpages = rest[npg:2 * npg]
    o_ref, qbd_ref, m_ref, l_ref, acc_ref = rest[2 * npg:]
    j = pl.program_id(1)
    rr = DIFF_REP * tn
    gr = 2 * rr

    @pl.when(j == 0)
    def _():
        _init_state(m_ref, l_ref, acc_ref)
        qbd_ref[...] = jnp.zeros(qbd_ref.shape, F32)
        q = q_ref[...]
        for g in range(DIFF_KV):
            for m in range(2):
                c = g * 2 + m
                for r in range(DIFF_REP):
                    col = g * 256 + r * 128 + m * DIFF_DH
                    qbd_ref[c * rr + r * tn:c * rr + (r + 1) * tn, c * DIFF_DH:(c + 1) * DIFF_DH] = q[:, col:col + DIFF_DH]

    qbd = qbd_ref[...].astype(BF16)
    kt = jnp.concatenate([p[...] for p in kpages], axis=1).astype(BF16)
    p, corr = _probs(jnp.dot(qbd, kt, preferred_element_type=F32), m_ref, l_ref)
    pv = []
    for g in range(DIFF_KV):
        vg = jnp.concatenate([vp[pl.ds(g, ps, stride=DIFF_KV), :] for vp in vpages], axis=0).astype(BF16)
        pv.append(jnp.dot(p[g * gr:(g + 1) * gr], vg, preferred_element_type=F32))
    acc_ref[...] = acc_ref[...] * corr + jnp.concatenate(pv, axis=0)

    @pl.when(j == pl.num_programs(1) - 1)
    def _():
        kn = _pad_new(knew_ref[...])
        vn = _pad_new(vnew_ref[...])
        pn, cn = _probs(_causal(_nt_dot(qbd, kn), tn), m_ref, l_ref)
        pvn = [jnp.dot(pn[g * gr:(g + 1) * gr], vn[:, g * DIFF_VD:(g + 1) * DIFF_VD], preferred_element_type=F32)
               for g in range(DIFF_KV)]
        acc = acc_ref[...] * cn + jnp.concatenate(pvn, axis=0)
        lam = _lambda(lq1_ref, lk1_ref, lq2_ref, lk2_ref, lam_init)
        o = acc / jnp.sum(l_ref[...], axis=1, keepdims=True)
        for g in range(DIFF_KV):
            for r in range(DIFF_REP):
                o0 = o[(2 * g) * rr + r * tn:(2 * g) * rr + (r + 1) * tn]
                o1 = o[(2 * g + 1) * rr + r * tn:(2 * g + 1) * rr + (r + 1) * tn]
                d = _rms(o0 - lam * o1, sub_ref[...], 1e-5) * (1.0 - lam_init)
                col = (g * DIFF_REP + r) * DIFF_VD
                o_ref[:, col:col + DIFF_VD] = d


def _diff_paged(page_table, dqr, dkr, zmix, lam_p, sub, cache_kt, cache_v2, layer, tn, lam_init):
    nb, n_pages = page_table.shape
    ps = cache_kt.shape[3]
    npg = _tile(n_pages, (16, 8, 4, 2, 1))
    assert tn & (tn - 1) == 0

    def kspec(i):
        return pl.BlockSpec((None, None, 256, ps), lambda b, j, pt: (layer, pt[b * n_pages + j * npg + i], 0, 0))

    def vspec(i):
        return pl.BlockSpec((None, None, DIFF_KV * ps, DIFF_VD),
                            lambda b, j, pt: (layer, pt[b * n_pages + j * npg + i], 0, 0))

    vec = lambda w: pl.BlockSpec((1, w), lambda b, j, pt: (0, 0))
    nrow = 4 * DIFF_REP * tn
    grid_spec = pltpu.PrefetchScalarGridSpec(
        num_scalar_prefetch=1,
        grid=(nb, n_pages // npg),
        in_specs=[pl.BlockSpec((tn, 512), lambda b, j, pt: (b, 0)),
                  pl.BlockSpec((tn, 256), lambda b, j, pt: (b, 0)),
                  pl.BlockSpec((tn, 256), lambda b, j, pt: (b, MIX_DV // 256)),
                  vec(DIFF_DH), vec(DIFF_DH), vec(DIFF_DH), vec(DIFF_DH), vec(DIFF_VD)]
                 + [kspec(i) for i in range(npg)] + [vspec(i) for i in range(npg)],
        out_specs=pl.BlockSpec((tn, BRANCH), lambda b, j, pt: (b, 0)),
        scratch_shapes=[pltpu.VMEM((nrow, 256), F32), pltpu.VMEM((nrow, LANES), F32), pltpu.VMEM((nrow, LANES), F32),
                        pltpu.VMEM((nrow, DIFF_VD), F32)],
    )
    return pl.pallas_call(
        functools.partial(_diff_paged_kernel, npg=npg, tn=tn, ps=ps, lam_init=lam_init),
        grid_spec=grid_spec,
        out_shape=jax.ShapeDtypeStruct((nb * tn, BRANCH), F32),
        compiler_params=_params("parallel", "arbitrary"),
        name="diff_paged",
    )(page_table.reshape(-1), dqr, dkr, zmix, *lam_p, sub, *([cache_kt] * npg), *([cache_v2] * npg))


CONV_HALO = 32
POOL_HALO = 16


def _seq_kernel(pool_ref, ga_ref, gg_ref, pc_ref, pp_ref, cw_ref, cb_ref, lg_ref, lb_ref, pw_ref, ps_ref, cnt_ref,
                bout_ref, cout_ref, nconv_ref, npool_ref, cwin_ref, pwin_ref, conv_ref, *, tc):
    i = pl.program_id(1)

    @pl.when(i == 0)
    def _():
        cwin_ref[0:CONV_HALO, :] = pc_ref[0]
        pwin_ref[0:POOL_HALO, :] = pp_ref[0]

    cwin_ref[CONV_HALO:CONV_HALO + tc, :] = ga_ref[...] * jax.nn.sigmoid(gg_ref[...])
    x = pool_ref[...]
    pwin_ref[POOL_HALO:POOL_HALO + tc, :] = x

    rc = min(tc, 64)
    off = CONV_HALO - CONV_PREV
    for r0 in range(0, tc, rc):
        for c0 in range(0, BRANCH, LANES):
            acc = jnp.broadcast_to(cb_ref[:, c0:c0 + LANES], (rc, LANES))
            for w in range(CONV_W):
                acc = acc + cwin_ref[off + r0 + w:off + r0 + w + rc, c0:c0 + LANES] * cw_ref[w:w + 1, c0:c0 + LANES]
            conv_ref[r0:r0 + rc, c0:c0 + LANES] = acc
    cv = conv_ref[...]
    mu = jnp.mean(cv, axis=-1, keepdims=True)
    var = jnp.mean(jnp.square(cv - mu), axis=-1, keepdims=True)
    y = (cv - mu) * lax.rsqrt(var + 1e-5) * lg_ref[...] + lb_ref[...]
    bout_ref[...] = y * jax.nn.sigmoid(y)

    for g, w in enumerate(POOL_WINDOWS):
        cs = slice(g * POOL_GROUP, (g + 1) * POOL_GROUP)
        s = pwin_ref[POOL_HALO:POOL_HALO + tc, cs]
        for jj in range(1, w):
            s = s + pwin_ref[POOL_HALO - jj:POOL_HALO - jj + tc, cs]
        pooled = (s / cnt_ref[:, cs] - x[:, cs]).astype(BF16)
        cout_ref[:, cs] = jnp.dot(pooled, pw_ref[g], preferred_element_type=F32) * ps_ref[:, cs]

    @pl.when(i == pl.num_programs(1) - 1)
    def _():
        nconv_ref[0] = cwin_ref[tc + off:tc + CONV_HALO, :]
        npool_ref[0] = pwin_ref[tc + 1:tc + POOL_HALO, :]

    carry_c = cwin_ref[tc:tc + CONV_HALO, :]
    carry_p = pwin_ref[tc:tc + POOL_HALO, :]
    cwin_ref[0:CONV_HALO, :] = carry_c
    pwin_ref[0:POOL_HALO, :] = carry_p


def _seq_mixers(zmix, prev_conv, prev_pool, cw, cb, lg, lb, pw, ps, cnt, b, t):
    tc = _tile(t, (128, 64, 32, 16, 8))
    nt = t // tc
    col = lambda c: pl.BlockSpec((tc, BRANCH), lambda bi, i: (bi * nt + i, c))
    full = lambda shape: pl.BlockSpec(shape, lambda bi, i: (0,) * len(shape))
    return pl.pallas_call(
        functools.partial(_seq_kernel, tc=tc),
        grid=(b, nt),
        in_specs=[col(MIX_POOL // BRANCH), col(MIX_GLU_A // BRANCH), col(MIX_GLU_G // BRANCH),
                  pl.BlockSpec((1, CONV_HALO, BRANCH), lambda bi, i: (bi, 0, 0)),
                  pl.BlockSpec((1, POOL_HALO, BRANCH), lambda bi, i: (bi, 0, 0)),
                  full((CONV_W, BRANCH)), full((1, BRANCH)), full((1, BRANCH)), full((1, BRANCH)),
                  full((len(POOL_WINDOWS), POOL_GROUP, POOL_GROUP)), full((1, BRANCH)),
                  pl.BlockSpec((tc, BRANCH), lambda bi, i: (i, 0))],
        out_specs=[pl.BlockSpec((tc, BRANCH), lambda bi, i: (bi * nt + i, 0)),
                   pl.BlockSpec((tc, BRANCH), lambda bi, i: (bi * nt + i, 0)),
                   pl.BlockSpec((1, CONV_PREV, BRANCH), lambda bi, i: (bi, 0, 0)),
                   pl.BlockSpec((1, POOL_PREV, BRANCH), lambda bi, i: (bi, 0, 0))],
        out_shape=[jax.ShapeDtypeStruct((b * t, BRANCH), F32), jax.ShapeDtypeStruct((b * t, BRANCH), F32),
                   jax.ShapeDtypeStruct((b, CONV_PREV, BRANCH), F32), jax.ShapeDtypeStruct((b, POOL_PREV, BRANCH), F32)],
        scratch_shapes=[pltpu.VMEM((CONV_HALO + tc, BRANCH), F32), pltpu.VMEM((POOL_HALO + tc, BRANCH), F32),
                        pltpu.VMEM((tc, BRANCH), F32)],
        compiler_params=_params("parallel", "arbitrary"),
        name="seq_mixers",
    )(zmix, zmix, zmix, prev_conv, prev_pool, cw, cb, lg, lb, pw, ps, cnt)


def _merge_kernel(g0, g1, g2, g3, a_ref, b_ref, c_ref, d_ref, wb_ref, o_ref):
    acc = None
    for n, (g, br) in enumerate(((g0, a_ref), (g1, b_ref), (g2, c_ref), (g3, d_ref))):
        up = jnp.dot(br[...].astype(BF16), wb_ref[n], preferred_element_type=F32)
        term = g[...] * up
        acc = term if acc is None else acc + term
    o_ref[...] = acc.astype(BF16)


def _merge(gates, branches, wb):
    n = gates.shape[0]
    tm = _tile(n, (512, 256))
    tn = 512
    nj = D_MODEL // tn
    gate = lambda k: pl.BlockSpec((tm, tn), lambda i, j: (i, k * nj + j))
    br = pl.BlockSpec((tm, BRANCH), lambda i, j: (i, 0))
    return pl.pallas_call(
        _merge_kernel,
        grid=(n // tm, nj),
        in_specs=[gate(0), gate(1), gate(2), gate(3), br, br, br, br,
                  pl.BlockSpec((N_BRANCH, BRANCH, tn), lambda i, j: (0, 0, j))],
        out_specs=pl.BlockSpec((tm, tn), lambda i, j: (i, j)),
        out_shape=jax.ShapeDtypeStruct((n, D_MODEL), BF16),
        compiler_params=_params("parallel", "arbitrary"),
        name="merge",
    )(gates, gates, gates, gates, *branches, wb)


def _split_bf16(x):
    hi = x.astype(BF16)
    return hi, (x - hi.astype(F32)).astype(BF16)


def _outproj_kernel(m_ref, x_ref, w_ref, g_ref, rwh_ref, rwl_ref, rb_ref, xo_ref, h_ref, combt_ref):
    xn = x_ref[...] + jnp.dot(m_ref[...], w_ref[...], preferred_element_type=F32)
    xo_ref[...] = xn
    t = _rms(xn, g_ref[...], EPS)
    h_ref[...] = t.astype(BF16)
    t_hi, t_lo = _split_bf16(t)
    logits = (jnp.dot(t_hi, rwh_ref[...], preferred_element_type=F32)
              + jnp.dot(t_lo, rwh_ref[...], preferred_element_type=F32)
              + jnp.dot(t_hi, rwl_ref[...], preferred_element_type=F32)) + rb_ref[...]
    lane = lax.broadcasted_iota(jnp.int32, logits.shape, 1).astype(F32)
    big = 1e9
    gl = jnp.where(lane < N_GROUPS, logits, -jnp.inf)
    gmax = jnp.max(gl, axis=-1, keepdims=True)
    gidx = jnp.min(jnp.where(gl == gmax, lane, big), axis=-1, keepdims=True)
    gw = 1.0 / jnp.sum(jnp.exp(gl - gmax), axis=-1, keepdims=True)
    lo = ROUTER_LANE0 + E_PER_GROUP * gidx
    el = jnp.where((lane >= lo) & (lane < lo + E_PER_GROUP), logits, -jnp.inf)
    v1 = jnp.max(el, axis=-1, keepdims=True)
    i1 = jnp.min(jnp.where(el == v1, lane, big), axis=-1, keepdims=True)
    el2 = jnp.where(lane == i1, -jnp.inf, el)
    v2 = jnp.max(el2, axis=-1, keepdims=True)
    i2 = jnp.min(jnp.where(el2 == v2, lane, big), axis=-1, keepdims=True)
    e2 = jnp.exp(v2 - v1)
    den = 1.0 + e2
    comb = jnp.where(lane == i1, (1.0 / den) * gw, 0.0) + jnp.where(lane == i2, (e2 / den) * gw, 0.0)
    combt_ref[...] = comb.T


def _outproj(merged, x, w_out, g, rwh, rwl, rb):
    n = x.shape[0]
    tm = _tile(n, (256, 128))
    full = lambda shape: pl.BlockSpec(shape, lambda i: (0,) * len(shape))
    row = lambda w: pl.BlockSpec((tm, w), lambda i: (i, 0))
    return pl.pallas_call(
        _outproj_kernel,
        grid=(n // tm,),
        in_specs=[row(D_MODEL), row(D_MODEL), full((D_MODEL, D_MODEL)), full((1, D_MODEL)),
                  full((D_MODEL, LANES)), full((D_MODEL, LANES)), full((1, LANES))],
        out_specs=[row(D_MODEL), row(D_MODEL), pl.BlockSpec((LANES, tm), lambda i: (0, i))],
        out_shape=[jax.ShapeDtypeStruct((n, D_MODEL), F32), jax.ShapeDtypeStruct((n, D_MODEL), BF16),
                   jax.ShapeDtypeStruct((LANES, n), F32)],
        compiler_params=_params("parallel"),
        name="outproj",
    )(merged, x, w_out, g, rwh, rwl, rb)


MOE_CAP = 128


def _moe_kernel(h_ref, ct_ref, x_ref, wg_ref, wu_ref, wd_ref, fg_ref, o_ref, acc_ref, rank_ref, *, final, th):
    e = pl.program_id(1)
    hh = pl.program_id(2)
    rows = pl.ds(pl.multiple_of(hh * th, th), th)

    @pl.when(e == 0)
    def _():
        acc_ref[rows, :] = jnp.zeros((th, D_MODEL), F32)
        sel = jnp.where(ct_ref[...] > 0.0, 1.0, 0.0).astype(BF16)
        before = (lax.broadcasted_iota(jnp.int32, (th, th), 0) < lax.broadcasted_iota(jnp.int32, (th, th), 1))
        rank_ref[hh] = jnp.dot(sel, jnp.where(before, 1.0, 0.0).astype(BF16), preferred_element_type=F32)

    row = ROUTER_LANE0 + e
    w_row = ct_ref[pl.ds(row, 1), :]
    r_row = rank_ref[hh, pl.ds(row, 1), :]
    sel_row = w_row > 0.0
    count = jnp.sum(jnp.where(sel_row, 1.0, 0.0), axis=1, keepdims=True)[0, 0].astype(jnp.int32)
    n_chunks = lax.div(count + (MOE_CAP - 1), MOE_CAP)

    def chunk(s, carry):
        slot = lax.broadcasted_iota(jnp.int32, (MOE_CAP, th), 0).astype(F32) + (s * MOE_CAP).astype(F32)
        pm = (slot == r_row) & sel_row
        pb = jnp.where(pm, 1.0, 0.0).astype(BF16)
        xc = jnp.dot(pb, h_ref[rows, :], preferred_element_type=F32).astype(BF16)
        wc = jnp.sum(jnp.where(pm, w_row, 0.0), axis=1, keepdims=True)
        gate = jnp.dot(xc, wg_ref[0], preferred_element_type=F32)
        up = jnp.dot(xc, wu_ref[0], preferred_element_type=F32)
        hid = (gate * jax.nn.sigmoid(gate) * up * wc).astype(BF16)
        y = jnp.dot(hid, wd_ref[0], preferred_element_type=F32)
        y_hi, y_lo = _split_bf16(y)
        back = lax.dot_general(jnp.concatenate([pb, pb], axis=0), jnp.concatenate([y_hi, y_lo], axis=0),
                               (((0,), (0,)), ((), ())), preferred_element_type=F32)
        acc_ref[rows, :] += back
        return carry

    lax.fori_loop(0, n_chunks, chunk, 0)

    @pl.when(e == pl.num_programs(1) - 1)
    def _():
        y = x_ref[...] + acc_ref[rows, :]
        o_ref[...] = _rms(y, fg_ref[...], EPS) if final else y


def _moe(h2, combt, x, wg, wu, wd, fg, final):
    n = x.shape[0]
    th = _tile(n, (512, 256))
    nh = 2 if n % (2 * th) == 0 else 1
    tb = nh * th
    last = N_EXPERTS - 1
    half = lambda i, e, hh: (nh * i + jnp.where(e == last, hh, 0), 0)
    return pl.pallas_call(
        functools.partial(_moe_kernel, final=final, th=th),
        grid=(n // tb, N_EXPERTS, nh),
        in_specs=[pl.BlockSpec((tb, D_MODEL), lambda i, e, hh: (i, 0)),
                  pl.BlockSpec((LANES, th), lambda i, e, hh: (0, nh * i + hh)),
                  pl.BlockSpec((th, D_MODEL), half),
                  pl.BlockSpec((1, D_MODEL, EXPERT_FF), lambda i, e, hh: (e, 0, 0)),
                  pl.BlockSpec((1, D_MODEL, EXPERT_FF), lambda i, e, hh: (e, 0, 0)),
                  pl.BlockSpec((1, EXPERT_FF, D_MODEL), lambda i, e, hh: (e, 0, 0)),
                  pl.BlockSpec((1, D_MODEL), lambda i, e, hh: (0, 0))],
        out_specs=pl.BlockSpec((th, D_MODEL), half),
        out_shape=jax.ShapeDtypeStruct((n, D_MODEL), F32),
        scratch_shapes=[pltpu.VMEM((tb, D_MODEL), F32), pltpu.VMEM((nh, LANES, th), F32)],
        compiler_params=_params("parallel", "arbitrary", "arbitrary"),
        name="moe",
    )(h2, combt, x, wg, wu, wd, fg)


def _layer_weights(l, norm_mix_g, w_in, mla_q_norm_g, mla_w_uq, mla_kv_norm_g, mla_w_uk, mla_w_uv,
                   conv_w, conv_b, conv_ln_g, conv_ln_b, pool_w, pool_scale,
                   diff_lq1, diff_lk1, diff_lq2, diff_lk2, diff_subln_g,
                   w_branch, w_out, norm_ffn_g, router_g_w, router_g_b, router_e_w, router_e_b,
                   moe_w_gate, moe_w_up, moe_w_down):
    wi = w_in[l]
    o = [0]
    for s in (Q_LORA, KV_LORA, MLA_ROPE, 2 * BRANCH, BRANCH, 512, 256, 256, GATE_COLS):
        o.append(o[-1] + s)
    sl = lambda k: wi[:, o[k]:o[k + 1]]
    glu = sl(3)
    w_mix = jnp.concatenate([sl(0), sl(1), sl(4), glu[:, :BRANCH], glu[:, BRANCH:], sl(5), sl(6), sl(7), sl(2),
                             jnp.zeros((D_MODEL, LANES - MLA_ROPE), F32)], axis=1).astype(BF16)
    uq = mla_w_uq[l]
    half = MLA_ROPE // 2
    wuq = jnp.concatenate([uq[:, :, :MLA_NOPE].reshape(Q_LORA, -1),
                           uq[:, :, MLA_NOPE:MLA_NOPE + half].reshape(Q_LORA, -1),
                           uq[:, :, MLA_NOPE + half:].reshape(Q_LORA, -1)], axis=1).astype(BF16)
    rw = jnp.concatenate([router_g_w[l], jnp.transpose(router_e_w[l], (1, 0, 2)).reshape(D_MODEL, N_EXPERTS),
                          jnp.zeros((D_MODEL, LANES - N_GROUPS - N_EXPERTS), F32)], axis=1)
    rb = jnp.concatenate([router_g_b[l], router_e_b[l].reshape(-1),
                          jnp.zeros((LANES - N_GROUPS - N_EXPERTS,), F32)])[None, :]
    rwh = rw.astype(BF16)
    rwl = (rw - rwh.astype(F32)).astype(BF16)
    return dict(
        norm_mix=norm_mix_g[l][None, :], w_mix=w_mix, w_gate=sl(8).astype(BF16),
        gq=mla_q_norm_g[l][None, :], gkv=mla_kv_norm_g[l][None, :], wuq=wuq,
        wuk=jnp.transpose(mla_w_uk[l], (1, 2, 0)).astype(BF16), wuv=jnp.transpose(mla_w_uv[l], (1, 0, 2)).astype(BF16),
        cw=conv_w[l], cb=conv_b[l][None, :], lg=conv_ln_g[l][None, :], lb=conv_ln_b[l][None, :],
        pw=pool_w[l].astype(BF16), ps=pool_scale[l][None, :],
        lam=(diff_lq1[l][None, :], diff_lk1[l][None, :], diff_lq2[l][None, :], diff_lk2[l][None, :]),
        sub=diff_subln_g[l][None, :],
        wb=w_branch[l].astype(BF16), w_out=w_out[l].astype(BF16), norm_ffn=norm_ffn_g[l][None, :],
        rwh=rwh, rwl=rwl, rb=rb,
        wg=moe_w_gate[l].astype(BF16), wu=moe_w_up[l].astype(BF16), wd=moe_w_down[l].astype(BF16))


def _pool_counts(pos):
    cols = [jnp.broadcast_to(jnp.minimum(w, pos + 1).astype(F32)[:, None], (pos.shape[0], POOL_GROUP))
            for w in POOL_WINDOWS]
    return jnp.concatenate(cols, axis=1)


def _tile_rows(tab, rows):
    return jnp.tile(tab, (rows // tab.shape[0], 1)) if tab.shape[0] < rows else tab


def _trunk(x, pos, b, t, prev_conv, prev_pool, attend, weights, final_g, depth):
    tab_rows = t if t >= 64 else 64 * t
    tabs = tuple(_tile_rows(tb, tab_rows) for tb in _rope_tables(pos))
    cnt = _pool_counts(pos)
    news = []
    for l in range(depth):
        p = weights[l]
        lam_init = 0.8 - 0.6 * math.exp(-0.3 * l)
        zmix, h = _inproj(x, p['norm_mix'], p['w_mix'])
        gates = _gates(h, p['w_gate'])
        qmla, rows, dqr, dkr = _prep(zmix, p['gq'], p['gkv'], p['wuq'], p['wuk'], tabs, attend.q_dtype)
        a_out, d_out = attend(l, p, qmla, rows, dqr, dkr, zmix, lam_init)
        pc = jnp.pad(prev_conv[l], ((0, 0), (CONV_HALO - CONV_PREV, 0), (0, 0)))
        pp = jnp.pad(prev_pool[l], ((0, 0), (POOL_HALO - POOL_PREV, 0), (0, 0)))
        b_out, c_out, new_conv, new_pool = _seq_mixers(zmix, pc, pp, p['cw'], p['cb'], p['lg'], p['lb'],
                                                       p['pw'], p['ps'], cnt, b, t)
        merged = _merge(gates, (a_out, b_out, c_out, d_out), p['wb'])
        x, h2, combt = _outproj(merged, x, p['w_out'], p['norm_ffn'], p['rwh'], p['rwl'], p['rb'])
        x = _moe(h2, combt, x, p['wg'], p['wu'], p['wd'], final_g, final=(l == depth - 1))
        news.append((rows.reshape(b, t, MLA_ROW), dkr.reshape(b, t, DIFF_KV, 2, DIFF_DH),
                     zmix[:, MIX_DV:MIX_DV + 256].reshape(b, t, DIFF_KV, DIFF_VD), new_conv, new_pool))
    stacked = [jnp.stack([nw[i] for nw in news], axis=0) for i in range(5)]
    return x.reshape(b, t, D_MODEL), stacked


class _PromptAttend:
    q_dtype = BF16

    def __init__(self, b, t):
        self.b, self.t = b, t

    def __call__(self, l, p, qmla, rows, dqr, dkr, zmix, lam_init):
        a_out = _mla_prompt(qmla, rows, p['wuv'], self.b, self.t)
        d_out = _diff_prompt(dqr, dkr, zmix, p['lam'], p['sub'], self.b, self.t, lam_init)
        return a_out, d_out


class _PagedAttend:
    q_dtype = F32

    def __init__(self, page_table, cache_mla, cache_k, cache_v, tn):
        depth, n_pool, ps = cache_mla.shape[:3]
        self.pt, self.tn = page_table, tn
        self.cm = jnp.transpose(cache_mla, (0, 1, 3, 2))
        self.ck = jnp.transpose(cache_k, (0, 1, 3, 4, 5, 2)).reshape(depth, n_pool, 2 * DIFF_KV * DIFF_DH, ps)
        self.cv = cache_v.reshape(depth, n_pool, ps * DIFF_KV, DIFF_VD)

    def __call__(self, l, p, qmla, rows, dqr, dkr, zmix, lam_init):
        a_out = _mla_paged(self.pt, qmla, rows, p['wuv'], self.cm, l, self.tn)
        d_out = _diff_paged(self.pt, dqr, dkr, zmix, p['lam'], p['sub'], self.ck, self.cv, l, self.tn, lam_init)
        return a_out, d_out


def kernel(x_prompt, x_sample, cache_mla_kv, cache_diff_k, cache_diff_v, state_conv, state_pool, page_table, norm_mix_g, w_in, mla_q_norm_g, mla_w_uq, mla_kv_norm_g, mla_w_uk, mla_w_uv, conv_w, conv_b, conv_ln_g, conv_ln_b, pool_w, pool_scale, diff_lq1, diff_lk1, diff_lq2, diff_lk2, diff_subln_g, w_branch, w_out, norm_ffn_g, router_g_w, router_g_b, router_e_w, router_e_b, moe_w_gate, moe_w_up, moe_w_down, final_norm_g):
    depth = w_in.shape[0]
    weights = [_layer_weights(l, norm_mix_g, w_in, mla_q_norm_g, mla_w_uq, mla_kv_norm_g, mla_w_uk, mla_w_uv,
                              conv_w, conv_b, conv_ln_g, conv_ln_b, pool_w, pool_scale,
                              diff_lq1, diff_lk1, diff_lq2, diff_lk2, diff_subln_g,
                              w_branch, w_out, norm_ffn_g, router_g_w, router_g_b, router_e_w, router_e_b,
                              moe_w_gate, moe_w_up, moe_w_down) for l in range(depth)]
    final_g = final_norm_g[None, :]

    bp, tp, _ = x_prompt.shape
    pos_p = jnp.arange(tp, dtype=jnp.int32)
    conv0 = jnp.zeros((depth, bp, CONV_PREV, BRANCH), F32)
    pool0 = jnp.zeros((depth, bp, POOL_PREV, BRANCH), F32)
    y_p, news_p = _trunk(x_prompt.reshape(bp * tp, D_MODEL), pos_p, bp, tp, conv0, pool0,
                         _PromptAttend(bp, tp), weights, final_g, depth)

    bs, ts, _ = x_sample.shape
    past_len = page_table.shape[1] * cache_mla_kv.shape[2]
    pos_s = past_len + jnp.arange(ts, dtype=jnp.int32)
    y_s, news_s = _trunk(x_sample.reshape(bs * ts, D_MODEL), pos_s, bs, ts, state_conv, state_pool,
                         _PagedAttend(page_table, cache_mla_kv, cache_diff_k, cache_diff_v, ts), weights, final_g, depth)

    return (y_p, y_s, *news_p, *news_s)
```

```python
import functools
import math

import jax
import jax.numpy as jnp
from jax import lax
from jax.experimental import pallas as pl
from jax.experimental.pallas import tpu as pltpu

F32 = jnp.float32
BF16 = jnp.bfloat16

D_MODEL = 2048
BRANCH = 512
N_BRANCH = 4
MLA_NOPE = 64
MLA_ROPE = 32
MLA_VH = 64
MLA_H = 8
Q_LORA = 384
KV_LORA = 128
MLA_ROW = KV_LORA + MLA_ROPE
MLA_THETA = 10000.0
LOG2E = math.log2(math.e)
MLA_SCALE = LOG2E / math.sqrt(MLA_NOPE + MLA_ROPE)
CONV_W = 31
CONV_PREV = CONV_W - 1
POOL_WINDOWS = (2, 4, 8, 16)
POOL_GROUP = 128
POOL_PREV = 15
DIFF_DH = 64
DIFF_VD = 128
DIFF_KV = 2
DIFF_REP = 2
DIFF_SCALE = LOG2E / math.sqrt(DIFF_DH)
ROT_DIM = 16
ROPE_THETA = 500000.0
N_GROUPS = 4
E_PER_GROUP = 4
N_EXPERTS = 16
EXPERT_FF = 512
EPS = 1e-6
NEG_INF = -1e30
LANES = 128

MIX_QC = 0
MIX_KVC = 384
MIX_POOL = 512
MIX_GLU_A = 1024
MIX_GLU_G = 1536
MIX_DQ = 2048
MIX_DK = 2560
MIX_DV = 2816
MIX_KROPE = 3072
MIX_COLS = 3200
GATE_COLS = N_BRANCH * D_MODEL
ROUTER_LANE0 = N_GROUPS

VMEM_LIMIT = 56 * 1024 * 1024


def _params(*sem):
    return pltpu.CompilerParams(dimension_semantics=sem, vmem_limit_bytes=VMEM_LIMIT)


def _tile(n, cands):
    for c in cands:
        if n % c == 0:
            return c
    raise ValueError(f"no tile for {n} in {cands}")


def _rms(x, g, eps):
    return x * lax.rsqrt(jnp.mean(x * x, axis=-1, keepdims=True) + eps) * g


def _inproj_kernel(x_ref, g_ref, w_ref, z_ref, h_ref):
    @pl.when(pl.program_id(1) == 0)
    def _():
        h_ref[...] = _rms(x_ref[...], g_ref[...], EPS).astype(BF16)

    z_ref[...] = jnp.dot(h_ref[...], w_ref[...], preferred_element_type=F32)


def _inproj(x, g, w_mix):
    n = x.shape[0]
    tm = _tile(n, (512, 256))
    tn = 640
    return pl.pallas_call(
        _inproj_kernel,
        grid=(n // tm, MIX_COLS // tn),
        in_specs=[pl.BlockSpec((tm, D_MODEL), lambda i, j: (i, 0)),
                  pl.BlockSpec((1, D_MODEL), lambda i, j: (0, 0)),
                  pl.BlockSpec((D_MODEL, tn), lambda i, j: (0, j))],
        out_specs=[pl.BlockSpec((tm, tn), lambda i, j: (i, j)),
                   pl.BlockSpec((tm, D_MODEL), lambda i, j: (i, 0))],
        out_shape=[jax.ShapeDtypeStruct((n, MIX_COLS), F32),
                   jax.ShapeDtypeStruct((n, D_MODEL), BF16)],
        compiler_params=_params("parallel", "arbitrary"),
        name="inproj",
    )(x, g, w_mix)


def _gates_kernel(h_ref, w_ref, o_ref):
    o_ref[...] = jax.nn.sigmoid(jnp.dot(h_ref[...], w_ref[...], preferred_element_type=F32))


def _gates(h, w_gate):
    n = h.shape[0]
    tm = _tile(n, (1024, 512, 256))
    tn = 1024
    return pl.pallas_call(
        _gates_kernel,
        grid=(n // tm, GATE_COLS // tn),
        in_specs=[pl.BlockSpec((tm, D_MODEL), lambda i, j: (i, 0)),
                  pl.BlockSpec((D_MODEL, tn), lambda i, j: (0, j))],
        out_specs=pl.BlockSpec((tm, tn), lambda i, j: (i, j)),
        out_shape=jax.ShapeDtypeStruct((n, GATE_COLS), F32),
        compiler_params=_params("parallel", "arbitrary"),
        name="gates",
    )(h, w_gate)


def _prep_kernel(qc_ref, kvc_ref, kr_ref, dq_ref, dk_ref, gq_ref, gkv_ref, wuq_ref, wuk_ref,
                 c8_ref, s8_ref, c16_ref, s16_ref, dc_ref, dlo_ref, dhi_ref,
                 qmla_ref, rows_ref, dqr_ref, dkr_ref):
    qn = _rms(qc_ref[...], gq_ref[...], EPS).astype(BF16)
    q = jnp.dot(qn, wuq_ref[...], preferred_element_type=F32)
    nope = q[:, :MLA_H * MLA_NOPE].astype(BF16)
    r1 = q[:, 512:640]
    r2 = q[:, 640:768]
    c8, s8 = c8_ref[...], s8_ref[...]
    pe1 = r1 * c8 - r2 * s8
    pe2 = r1 * s8 + r2 * c8
    half = MLA_ROPE // 2
    for h in range(MLA_H):
        lat = jnp.dot(nope[:, h * MLA_NOPE:(h + 1) * MLA_NOPE], wuk_ref[h], preferred_element_type=F32)
        qh = jnp.concatenate([lat, pe1[:, h * half:(h + 1) * half], pe2[:, h * half:(h + 1) * half]], axis=-1)
        qmla_ref[h] = (qh * MLA_SCALE).astype(qmla_ref.dtype)
    ckv = _rms(kvc_ref[...], gkv_ref[...], EPS)
    kr = kr_ref[...]
    k1 = kr[:, :half]
    k2 = kr[:, half:MLA_ROPE]
    c16, s16 = c16_ref[...], s16_ref[...]
    rows_ref[...] = jnp.concatenate([ckv, k1 * c16 - k2 * s16, k1 * s16 + k2 * c16], axis=-1)
    dc, dlo, dhi = dc_ref[...], dlo_ref[...], dhi_ref[...]

    def rot(x):
        return x * dc + pltpu.roll(x, LANES - ROT_DIM // 2, 1) * dlo + pltpu.roll(x, ROT_DIM // 2, 1) * dhi

    for c in range(4):
        sl = slice(c * LANES, (c + 1) * LANES)
        dqr_ref[:, sl] = (rot(dq_ref[:, sl]) * DIFF_SCALE).astype(dqr_ref.dtype)
    for c in range(2):
        sl = slice(c * LANES, (c + 1) * LANES)
        dkr_ref[:, sl] = rot(dk_ref[:, sl])


def _prep(zmix, gq, gkv, wuq, wuk, tabs, q_dtype):
    n = zmix.shape[0]
    tab_rows = tabs[0].shape[0]
    tm = _tile(math.gcd(n, tab_rows), (512, 256, 128, 64))
    nt = tab_rows // tm
    row = lambda w, c: pl.BlockSpec((tm, w), lambda i: (i, c))
    tab = lambda w: pl.BlockSpec((tm, w), lambda i: (i % nt, 0))
    full = lambda shape: pl.BlockSpec(shape, lambda i: (0,) * len(shape))
    return pl.pallas_call(
        _prep_kernel,
        grid=(n // tm,),
        in_specs=[row(Q_LORA, MIX_QC // Q_LORA), row(KV_LORA, MIX_KVC // KV_LORA), row(LANES, MIX_KROPE // LANES),
                  row(512, MIX_DQ // 512), row(256, MIX_DK // 256),
                  full((1, Q_LORA)), full((1, KV_LORA)), full((Q_LORA, 768)), full((MLA_H, MLA_NOPE, KV_LORA)),
                  tab(LANES), tab(LANES), tab(16), tab(16), tab(LANES), tab(LANES), tab(LANES)],
        out_specs=[pl.BlockSpec((MLA_H, tm, MLA_ROW), lambda i: (0, i, 0)),
                   pl.BlockSpec((tm, MLA_ROW), lambda i: (i, 0)),
                   pl.BlockSpec((tm, 512), lambda i: (i, 0)),
                   pl.BlockSpec((tm, 256), lambda i: (i, 0))],
        out_shape=[jax.ShapeDtypeStruct((MLA_H, n, MLA_ROW), q_dtype),
                   jax.ShapeDtypeStruct((n, MLA_ROW), F32),
                   jax.ShapeDtypeStruct((n, 512), q_dtype),
                   jax.ShapeDtypeStruct((n, 256), F32)],
        compiler_params=_params("parallel"),
        name="prep",
    )(zmix, zmix, zmix, zmix, zmix, gq, gkv, wuq, wuk, *tabs)


def _rope_tables(pos):
    posf = pos.astype(F32)[:, None]
    inv_m = 1.0 / (MLA_THETA ** (jnp.arange(0, MLA_ROPE, 2, dtype=F32) / MLA_ROPE))
    ang_m = posf * inv_m[None, :]
    c16, s16 = jnp.cos(ang_m), jnp.sin(ang_m)
    c8, s8 = jnp.tile(c16, (1, MLA_H)), jnp.tile(s16, (1, MLA_H))
    inv_d = 1.0 / (ROPE_THETA ** (jnp.arange(0, ROT_DIM, 2, dtype=F32) / ROT_DIM))
    ang_d = posf * inv_d[None, :]
    cd, sd = jnp.cos(ang_d), jnp.sin(ang_d)
    t = pos.shape[0]
    hr = ROT_DIM // 2
    one = jnp.ones((t, DIFF_DH - ROT_DIM), F32)
    zero = jnp.zeros((t, DIFF_DH - ROT_DIM), F32)
    z8 = jnp.zeros((t, hr), F32)
    dc = jnp.tile(jnp.concatenate([cd, cd, one], axis=1), (1, 2))
    dlo = jnp.tile(jnp.concatenate([-sd, z8, zero], axis=1), (1, 2))
    dhi = jnp.tile(jnp.concatenate([z8, sd, zero], axis=1), (1, 2))
    return c8, s8, c16, s16, dc, dlo, dhi


def _nt_dot(a, b):
    return lax.dot_general(a, b, (((1,), (1,)), ((), ())), preferred_element_type=F32)


def _init_state(m_ref, l_ref, acc_ref):
    m_ref[...] = jnp.full(m_ref.shape, NEG_INF, F32)
    l_ref[...] = jnp.zeros(l_ref.shape, F32)
    acc_ref[...] = jnp.zeros(acc_ref.shape, F32)


def _probs(s, m_ref, l_ref):
    m_prev = m_ref[...]
    m_new = jnp.maximum(m_prev, jnp.max(s, axis=1, keepdims=True))
    corr = jnp.exp2(m_prev - m_new)
    ps = [jnp.exp2(s[:, c:c + LANES] - m_new) for c in range(0, s.shape[1], LANES)]
    lsum = ps[0]
    for p in ps[1:]:
        lsum = lsum + p
    l_ref[...] = l_ref[...] * corr + lsum
    m_ref[...] = m_new
    p = ps[0] if len(ps) == 1 else jnp.concatenate(ps, axis=1)
    return p.astype(BF16), corr


def _causal(s, tq):
    t = lax.broadcasted_iota(jnp.int32, s.shape, 0) & (tq - 1)
    c = lax.broadcasted_iota(jnp.int32, s.shape, 1)
    return jnp.where(c <= t, s, NEG_INF)


def _tri_pairs(nq):
    qi = [q for q in range(nq) for _ in range(q + 1)]
    ki = [k for q in range(nq) for k in range(q + 1)]
    return jnp.asarray(qi, jnp.int32), jnp.asarray(ki, jnp.int32)


def _mla_prompt_kernel(qi_ref, ki_ref, q_ref, k_ref, wuv_ref, o_ref, m_ref, l_ref, acc_ref, *, tq):
    pair = pl.program_id(1)
    qi, ki = qi_ref[pair], ki_ref[pair]

    @pl.when(ki == 0)
    def _():
        _init_state(m_ref, l_ref, acc_ref)

    def step(masked):
        k = k_ref[...].astype(BF16)
        v = k[:, :KV_LORA]
        for h in range(MLA_H):
            rs = slice(h * tq, (h + 1) * tq)
            s = _nt_dot(q_ref[h], k)
            if masked:
                s = _causal(s, tq)
            p, corr = _probs(s, m_ref.at[rs, :], l_ref.at[rs, :])
            acc_ref[rs, :] = acc_ref[rs, :] * corr + jnp.dot(p, v, preferred_element_type=F32)

    @pl.when(ki < qi)
    def _():
        step(False)

    @pl.when(ki == qi)
    def _():
        step(True)
        outs = []
        for h in range(MLA_H):
            rs = slice(h * tq, (h + 1) * tq)
            o = acc_ref[rs, :] / jnp.sum(l_ref[rs, :], axis=1, keepdims=True)
            outs.append(jnp.dot(o.astype(BF16), wuv_ref[h], preferred_element_type=F32))
        o_ref[...] = jnp.concatenate(outs, axis=-1)


def _mla_prompt(qmla, rows, wuv, b, t):
    tq = _tile(t, (256, 128))
    assert tq & (tq - 1) == 0
    nq = t // tq
    qi_tab, ki_tab = _tri_pairs(nq)
    grid_spec = pltpu.PrefetchScalarGridSpec(
        num_scalar_prefetch=2,
        grid=(b, qi_tab.shape[0]),
        in_specs=[pl.BlockSpec((MLA_H, tq, MLA_ROW), lambda bi, p, qt, kt: (0, bi * nq + qt[p], 0)),
                  pl.BlockSpec((tq, MLA_ROW), lambda bi, p, qt, kt: (bi * nq + kt[p], 0)),
                  pl.BlockSpec((MLA_H, KV_LORA, MLA_VH), lambda bi, p, qt, kt: (0, 0, 0))],
        out_specs=pl.BlockSpec((tq, BRANCH), lambda bi, p, qt, kt: (bi * nq + qt[p], 0)),
        scratch_shapes=[pltpu.VMEM((MLA_H * tq, LANES), F32), pltpu.VMEM((MLA_H * tq, LANES), F32),
                        pltpu.VMEM((MLA_H * tq, KV_LORA), F32)],
    )
    return pl.pallas_call(
        functools.partial(_mla_prompt_kernel, tq=tq),
        grid_spec=grid_spec,
        out_shape=jax.ShapeDtypeStruct((b * t, BRANCH), F32),
        compiler_params=_params("parallel", "arbitrary"),
        name="mla_prompt",
    )(qi_tab, ki_tab, qmla, rows, wuv)


def _lambda(lq1_ref, lk1_ref, lq2_ref, lk2_ref, lam_init):
    a = jnp.exp(jnp.sum(lq1_ref[...] * lk1_ref[...], axis=-1, keepdims=True))
    b = jnp.exp(jnp.sum(lq2_ref[...] * lk2_ref[...], axis=-1, keepdims=True))
    return a - b + lam_init


def _diff_prompt_kernel(qi_ref, ki_ref, q_ref, k_ref, v_ref, lq1_ref, lk1_ref, lq2_ref, lk2_ref, sub_ref, o_ref,
                        qs_ref, m_ref, l_ref, acc_ref, *, tq, lam_init):
    pair = pl.program_id(1)
    qi, ki = qi_ref[pair], ki_ref[pair]
    rr = DIFF_REP * tq

    @pl.when(ki == 0)
    def _():
        _init_state(m_ref, l_ref, acc_ref)
        q = q_ref[...]
        for g in range(DIFF_KV):
            for m in range(2):
                c = g * 2 + m
                for r in range(DIFF_REP):
                    col = g * 256 + r * 128 + m * DIFF_DH
                    qs_ref[c * rr + r * tq:c * rr + (r + 1) * tq, :] = q[:, col:col + DIFF_DH]

    def step(masked):
        k = k_ref[...].astype(BF16)
        v = v_ref[...].astype(BF16)
        for c in range(4):
            g = c // 2
            rs = slice(c * rr, (c + 1) * rr)
            s = _nt_dot(qs_ref[rs, :], k[:, c * DIFF_DH:(c + 1) * DIFF_DH])
            if masked:
                s = _causal(s, tq)
            p, corr = _probs(s, m_ref.at[rs, :], l_ref.at[rs, :])
            acc_ref[rs, :] = acc_ref[rs, :] * corr + jnp.dot(p, v[:, g * DIFF_VD:(g + 1) * DIFF_VD],
                                                             preferred_element_type=F32)

    @pl.when(ki < qi)
    def _():
        step(False)

    @pl.when(ki == qi)
    def _():
        step(True)
        lam = _lambda(lq1_ref, lk1_ref, lq2_ref, lk2_ref, lam_init)
        o = acc_ref[...] / jnp.sum(l_ref[...], axis=1, keepdims=True)
        for g in range(DIFF_KV):
            for r in range(DIFF_REP):
                o0 = o[(2 * g) * rr + r * tq:(2 * g) * rr + (r + 1) * tq]
                o1 = o[(2 * g + 1) * rr + r * tq:(2 * g + 1) * rr + (r + 1) * tq]
                d = _rms(o0 - lam * o1, sub_ref[...], 1e-5) * (1.0 - lam_init)
                col = (g * DIFF_REP + r) * DIFF_VD
                o_ref[:, col:col + DIFF_VD] = d


def _diff_prompt(dqr, dkr, zmix, lam_p, sub, b, t, lam_init):
    tq = _tile(t, (256, 128))
    assert tq & (tq - 1) == 0
    nq = t // tq
    qi_tab, ki_tab = _tri_pairs(nq)
    vec = lambda w: pl.BlockSpec((1, w), lambda bi, p, qt, kt: (0, 0))
    nrow = 4 * DIFF_REP * tq
    grid_spec = pltpu.PrefetchScalarGridSpec(
        num_scalar_prefetch=2,
        grid=(b, qi_tab.shape[0]),
        in_specs=[pl.BlockSpec((tq, 512), lambda bi, p, qt, kt: (bi * nq + qt[p], 0)),
                  pl.BlockSpec((tq, 256), lambda bi, p, qt, kt: (bi * nq + kt[p], 0)),
                  pl.BlockSpec((tq, 256), lambda bi, p, qt, kt: (bi * nq + kt[p], MIX_DV // 256)),
                  vec(DIFF_DH), vec(DIFF_DH), vec(DIFF_DH), vec(DIFF_DH), vec(DIFF_VD)],
        out_specs=pl.BlockSpec((tq, BRANCH), lambda bi, p, qt, kt: (bi * nq + qt[p], 0)),
        scratch_shapes=[pltpu.VMEM((nrow, DIFF_DH), BF16), pltpu.VMEM((nrow, LANES), F32),
                        pltpu.VMEM((nrow, LANES), F32), pltpu.VMEM((nrow, DIFF_VD), F32)],
    )
    return pl.pallas_call(
        functools.partial(_diff_prompt_kernel, tq=tq, lam_init=lam_init),
        grid_spec=grid_spec,
        out_shape=jax.ShapeDtypeStruct((b * t, BRANCH), F32),
        compiler_params=_params("parallel", "arbitrary"),
        name="diff_prompt",
    )(qi_tab, ki_tab, dqr, dkr, zmix, *lam_p, sub)


def _pad_new(x):
    pad = jnp.zeros((LANES - x.shape[0], x.shape[1]), x.dtype)
    return jnp.concatenate([x, pad], axis=0).astype(BF16)


def _page_pipeline(pt_ref, streams, npg, layer):
    step = pl.program_id(0) * pl.num_programs(1) + pl.program_id(1)
    total = pl.num_programs(0) * pl.num_programs(1)
    slot = step & 1

    def copies(st, sl):
        return [pltpu.make_async_copy(cache.at[layer, pt_ref[st * npg + i]], buf.at[sl, i], sem.at[sl])
                for cache, buf, sem in streams for i in range(npg)]

    @pl.when(step == 0)
    def _():
        for c in copies(0, 0):
            c.start()

    @pl.when(step + 1 < total)
    def _():
        for c in copies(step + 1, 1 - slot):
            c.start()

    for c in copies(step, slot):
        c.wait()
    return slot


def _mla_paged_kernel(pt_ref, q_ref, knew_ref, wuv_ref, cache_ref, o_ref, buf_ref, sem_ref, m_ref, l_ref, acc_ref,
                      *, npg, tn, layer):
    j = pl.program_id(1)
    slot = _page_pipeline(pt_ref, ((cache_ref, buf_ref, sem_ref),), npg, layer)

    @pl.when(j == 0)
    def _():
        _init_state(m_ref, l_ref, acc_ref)

    q = q_ref[...].reshape(MLA_H * tn, MLA_ROW).astype(BF16)
    kt = jnp.concatenate([buf_ref[slot, i] for i in range(npg)], axis=1).astype(BF16)
    p, corr = _probs(jnp.dot(q, kt, preferred_element_type=F32), m_ref, l_ref)
    acc_ref[...] = acc_ref[...] * corr + _nt_dot(p, kt[:KV_LORA, :])

    @pl.when(j == pl.num_programs(1) - 1)
    def _():
        kn = _pad_new(knew_ref[...])
        pn, cn = _probs(_causal(_nt_dot(q, kn), tn), m_ref, l_ref)
        acc = acc_ref[...] * cn + jnp.dot(pn, kn[:, :KV_LORA], preferred_element_type=F32)
        o = acc / jnp.sum(l_ref[...], axis=1, keepdims=True)
        outs = [jnp.dot(o[h * tn:(h + 1) * tn].astype(BF16), wuv_ref[h], preferred_element_type=F32)
                for h in range(MLA_H)]
        o_ref[...] = jnp.concatenate(outs, axis=-1)


def _mla_paged(page_table, qmla, rows, wuv, cache_t, layer, tn):
    nb, n_pages = page_table.shape
    ps = cache_t.shape[3]
    npg = _tile(n_pages, (32, 16, 8, 4, 2, 1))
    assert tn & (tn - 1) == 0

    grid_spec = pltpu.PrefetchScalarGridSpec(
        num_scalar_prefetch=1,
        grid=(nb, n_pages // npg),
        in_specs=[pl.BlockSpec((MLA_H, tn, MLA_ROW), lambda b, j, pt: (0, b, 0)),
                  pl.BlockSpec((tn, MLA_ROW), lambda b, j, pt: (b, 0)),
                  pl.BlockSpec((MLA_H, KV_LORA, MLA_VH), lambda b, j, pt: (0, 0, 0)),
                  pl.BlockSpec(memory_space=pl.ANY)],
        out_specs=pl.BlockSpec((tn, BRANCH), lambda b, j, pt: (b, 0)),
        scratch_shapes=[pltpu.VMEM((2, npg, MLA_ROW, ps), F32), pltpu.SemaphoreType.DMA((2,)),
                        pltpu.VMEM((MLA_H * tn, LANES), F32), pltpu.VMEM((MLA_H * tn, LANES), F32),
                        pltpu.VMEM((MLA_H * tn, KV_LORA), F32)],
    )
    return pl.pallas_call(
        functools.partial(_mla_paged_kernel, npg=npg, tn=tn, layer=layer),
        grid_spec=grid_spec,
        out_shape=jax.ShapeDtypeStruct((nb * tn, BRANCH), F32),
        compiler_params=_params("arbitrary", "arbitrary"),
        name="mla_paged",
    )(page_table.reshape(-1), qmla, rows, wuv, cache_t)


def _diff_paged_kernel(pt_ref, q_ref, knew_ref, vnew_ref, lq1_ref, lk1_ref, lq2_ref, lk2_ref, sub_ref,
                       ck_ref, cv_ref, o_ref, kbuf_ref, vbuf_ref, ksem_ref, vsem_ref, qbd_ref, m_ref, l_ref, acc_ref,
                       *, npg, tn, ps, layer, lam_init):
    j = pl.program_id(1)
    rr = DIFF_REP * tn
    gr = 2 * rr
    slot = _page_pipeline(pt_ref, ((ck_ref, kbuf_ref, ksem_ref), (cv_ref, vbuf_ref, vsem_ref)), npg, layer)

    @pl.when(j == 0)
    def _():
        _init_state(m_ref, l_ref, acc_ref)
        qbd_ref[...] = jnp.zeros(qbd_ref.shape, F32)
        q = q_ref[...]
        for g in range(DIFF_KV):
            for m in range(2):
                c = g * 2 + m
                for r in range(DIFF_REP):
                    col = g * 256 + r * 128 + m * DIFF_DH
                    qbd_ref[c * rr + r * tn:c * rr + (r + 1) * tn, c * DIFF_DH:(c + 1) * DIFF_DH] = q[:, col:col + DIFF_DH]

    qbd = qbd_ref[...].astype(BF16)
    kt = jnp.concatenate([kbuf_ref[slot, i] for i in range(npg)], axis=1).astype(BF16)
    p, corr = _probs(jnp.dot(qbd, kt, preferred_element_type=F32), m_ref, l_ref)
    pv = []
    for g in range(DIFF_KV):
        vg = jnp.concatenate([vbuf_ref[slot, i, pl.ds(g, ps, stride=DIFF_KV), :] for i in range(npg)],
                             axis=0).astype(BF16)
        pv.append(jnp.dot(p[g * gr:(g + 1) * gr], vg, preferred_element_type=F32))
    acc_ref[...] = acc_ref[...] * corr + jnp.concatenate(pv, axis=0)

    @pl.when(j == pl.num_programs(1) - 1)
    def _():
        kn = _pad_new(knew_ref[...])
        vn = _pad_new(vnew_ref[...])
        pn, cn = _probs(_causal(_nt_dot(qbd, kn), tn), m_ref, l_ref)
        pvn = [jnp.dot(pn[g * gr:(g + 1) * gr], vn[:, g * DIFF_VD:(g + 1) * DIFF_VD], preferred_element_type=F32)
               for g in range(DIFF_KV)]
        acc = acc_ref[...] * cn + jnp.concatenate(pvn, axis=0)
        lam = _lambda(lq1_ref, lk1_ref, lq2_ref, lk2_ref, lam_init)
        o = acc / jnp.sum(l_ref[...], axis=1, keepdims=True)
        for g in range(DIFF_KV):
            for r in range(DIFF_REP):
                o0 = o[(2 * g) * rr + r * tn:(2 * g) * rr + (r + 1) * tn]
                o1 = o[(2 * g + 1) * rr + r * tn:(2 * g + 1) * rr + (r + 1) * tn]
                d = _rms(o0 - lam * o1, sub_ref[...], 1e-5) * (1.0 - lam_init)
                col = (g * DIFF_REP + r) * DIFF_VD
                o_ref[:, col:col + DIFF_VD] = d


def _diff_paged(page_table, dqr, dkr, zmix, lam_p, sub, cache_kt, cache_v2, layer, tn, lam_init):
    nb, n_pages = page_table.shape
    ps = cache_kt.shape[3]
    npg = _tile(n_pages, (32, 16, 8, 4, 2, 1))
    assert tn & (tn - 1) == 0
    vec = lambda w: pl.BlockSpec((1, w), lambda b, j, pt: (0, 0))
    nrow = 4 * DIFF_REP * tn
    grid_spec = pltpu.PrefetchScalarGridSpec(
        num_scalar_prefetch=1,
        grid=(nb, n_pages // npg),
        in_specs=[pl.BlockSpec((tn, 512), lambda b, j, pt: (b, 0)),
                  pl.BlockSpec((tn, 256), lambda b, j, pt: (b, 0)),
                  pl.BlockSpec((tn, 256), lambda b, j, pt: (b, MIX_DV // 256)),
                  vec(DIFF_DH), vec(DIFF_DH), vec(DIFF_DH), vec(DIFF_DH), vec(DIFF_VD),
                  pl.BlockSpec(memory_space=pl.ANY), pl.BlockSpec(memory_space=pl.ANY)],
        out_specs=pl.BlockSpec((tn, BRANCH), lambda b, j, pt: (b, 0)),
        scratch_shapes=[pltpu.VMEM((2, npg, 256, ps), F32), pltpu.VMEM((2, npg, DIFF_KV * ps, DIFF_VD), F32),
                        pltpu.SemaphoreType.DMA((2,)), pltpu.SemaphoreType.DMA((2,)),
                        pltpu.VMEM((nrow, 256), F32), pltpu.VMEM((nrow, LANES), F32), pltpu.VMEM((nrow, LANES), F32),
                        pltpu.VMEM((nrow, DIFF_VD), F32)],
    )
    return pl.pallas_call(
        functools.partial(_diff_paged_kernel, npg=npg, tn=tn, ps=ps, layer=layer, lam_init=lam_init),
        grid_spec=grid_spec,
        out_shape=jax.ShapeDtypeStruct((nb * tn, BRANCH), F32),
        compiler_params=_params("arbitrary", "arbitrary"),
        name="diff_paged",
    )(page_table.reshape(-1), dqr, dkr, zmix, *lam_p, sub, cache_kt, cache_v2)


CONV_HALO = 32
POOL_HALO = 16


def _seq_kernel(pool_ref, ga_ref, gg_ref, pc_ref, pp_ref, cw_ref, cb_ref, lg_ref, lb_ref, pw_ref, ps_ref, cnt_ref,
                bout_ref, cout_ref, nconv_ref, npool_ref, cwin_ref, pwin_ref, conv_ref, *, tc):
    i = pl.program_id(1)

    @pl.when(i == 0)
    def _():
        cwin_ref[0:CONV_HALO, :] = pc_ref[0]
        pwin_ref[0:POOL_HALO, :] = pp_ref[0]

    cwin_ref[CONV_HALO:CONV_HALO + tc, :] = ga_ref[...] * jax.nn.sigmoid(gg_ref[...])
    x = pool_ref[...]
    pwin_ref[POOL_HALO:POOL_HALO + tc, :] = x

    rc = min(tc, 64)
    off = CONV_HALO - CONV_PREV
    for r0 in range(0, tc, rc):
        for c0 in range(0, BRANCH, LANES):
            acc = jnp.broadcast_to(cb_ref[:, c0:c0 + LANES], (rc, LANES))
            for w in range(CONV_W):
                acc = acc + cwin_ref[off + r0 + w:off + r0 + w + rc, c0:c0 + LANES] * cw_ref[w:w + 1, c0:c0 + LANES]
            conv_ref[r0:r0 + rc, c0:c0 + LANES] = acc
    cv = conv_ref[...]
    mu = jnp.mean(cv, axis=-1, keepdims=True)
    var = jnp.mean(jnp.square(cv - mu), axis=-1, keepdims=True)
    y = (cv - mu) * lax.rsqrt(var + 1e-5) * lg_ref[...] + lb_ref[...]
    bout_ref[...] = y * jax.nn.sigmoid(y)

    for g, w in enumerate(POOL_WINDOWS):
        cs = slice(g * POOL_GROUP, (g + 1) * POOL_GROUP)
        s = pwin_ref[POOL_HALO:POOL_HALO + tc, cs]
        for jj in range(1, w):
            s = s + pwin_ref[POOL_HALO - jj:POOL_HALO - jj + tc, cs]
        pooled = (s / cnt_ref[:, cs] - x[:, cs]).astype(BF16)
        cout_ref[:, cs] = jnp.dot(pooled, pw_ref[g], preferred_element_type=F32) * ps_ref[:, cs]

    @pl.when(i == pl.num_programs(1) - 1)
    def _():
        nconv_ref[0] = cwin_ref[tc + off:tc + CONV_HALO, :]
        npool_ref[0] = pwin_ref[tc + 1:tc + POOL_HALO, :]

    carry_c = cwin_ref[tc:tc + CONV_HALO, :]
    carry_p = pwin_ref[tc:tc + POOL_HALO, :]
    cwin_ref[0:CONV_HALO, :] = carry_c
    pwin_ref[0:POOL_HALO, :] = carry_p


def _seq_mixers(zmix, prev_conv, prev_pool, cw, cb, lg, lb, pw, ps, cnt, b, t):
    tc = _tile(t, (128, 64, 32, 16, 8))
    nt = t // tc
    col = lambda c: pl.BlockSpec((tc, BRANCH), lambda bi, i: (bi * nt + i, c))
    full = lambda shape: pl.BlockSpec(shape, lambda bi, i: (0,) * len(shape))
    return pl.pallas_call(
        functools.partial(_seq_kernel, tc=tc),
        grid=(b, nt),
        in_specs=[col(MIX_POOL // BRANCH), col(MIX_GLU_A // BRANCH), col(MIX_GLU_G // BRANCH),
                  pl.BlockSpec((1, CONV_HALO, BRANCH), lambda bi, i: (bi, 0, 0)),
                  pl.BlockSpec((1, POOL_HALO, BRANCH), lambda bi, i: (bi, 0, 0)),
                  full((CONV_W, BRANCH)), full((1, BRANCH)), full((1, BRANCH)), full((1, BRANCH)),
                  full((len(POOL_WINDOWS), POOL_GROUP, POOL_GROUP)), full((1, BRANCH)),
                  pl.BlockSpec((tc, BRANCH), lambda bi, i: (i, 0))],
        out_specs=[pl.BlockSpec((tc, BRANCH), lambda bi, i: (bi * nt + i, 0)),
                   pl.BlockSpec((tc, BRANCH), lambda bi, i: (bi * nt + i, 0)),
                   pl.BlockSpec((1, CONV_PREV, BRANCH), lambda bi, i: (bi, 0, 0)),
                   pl.BlockSpec((1, POOL_PREV, BRANCH), lambda bi, i: (bi, 0, 0))],
        out_shape=[jax.ShapeDtypeStruct((b * t, BRANCH), F32), jax.ShapeDtypeStruct((b * t, BRANCH), F32),
                   jax.ShapeDtypeStruct((b, CONV_PREV, BRANCH), F32), jax.ShapeDtypeStruct((b, POOL_PREV, BRANCH), F32)],
        scratch_shapes=[pltpu.VMEM((CONV_HALO + tc, BRANCH), F32), pltpu.VMEM((POOL_HALO + tc, BRANCH), F32),
                        pltpu.VMEM((tc, BRANCH), F32)],
        compiler_params=_params("parallel", "arbitrary"),
        name="seq_mixers",
    )(zmix, zmix, zmix, prev_conv, prev_pool, cw, cb, lg, lb, pw, ps, cnt)


def _merge_kernel(g0, g1, g2, g3, a_ref, b_ref, c_ref, d_ref, wb_ref, o_ref):
    acc = None
    for n, (g, br) in enumerate(((g0, a_ref), (g1, b_ref), (g2, c_ref), (g3, d_ref))):
        up = jnp.dot(br[...].astype(BF16), wb_ref[n], preferred_element_type=F32)
        term = g[...] * up
        acc = term if acc is None else acc + term
    o_ref[...] = acc.astype(BF16)


def _merge(gates, branches, wb):
    n = gates.shape[0]
    tm = _tile(n, (512, 256))
    tn = 512
    nj = D_MODEL // tn
    gate = lambda k: pl.BlockSpec((tm, tn), lambda i, j: (i, k * nj + j))
    br = pl.BlockSpec((tm, BRANCH), lambda i, j: (i, 0))
    return pl.pallas_call(
        _merge_kernel,
        grid=(n // tm, nj),
        in_specs=[gate(0), gate(1), gate(2), gate(3), br, br, br, br,
                  pl.BlockSpec((N_BRANCH, BRANCH, tn), lambda i, j: (0, 0, j))],
        out_specs=pl.BlockSpec((tm, tn), lambda i, j: (i, j)),
        out_shape=jax.ShapeDtypeStruct((n, D_MODEL), BF16),
        compiler_params=_params("parallel", "arbitrary"),
        name="merge",
    )(gates, gates, gates, gates, *branches, wb)


def _split_bf16(x):
    hi = x.astype(BF16)
    return hi, (x - hi.astype(F32)).astype(BF16)


def _outproj_kernel(m_ref, x_ref, w_ref, g_ref, rwh_ref, rwl_ref, rb_ref, xo_ref, h_ref, combt_ref):
    xn = x_ref[...] + jnp.dot(m_ref[...], w_ref[...], preferred_element_type=F32)
    xo_ref[...] = xn
    t = _rms(xn, g_ref[...], EPS)
    h_ref[...] = t.astype(BF16)
    t_hi, t_lo = _split_bf16(t)
    logits = (jnp.dot(t_hi, rwh_ref[...], preferred_element_type=F32)
              + jnp.dot(t_lo, rwh_ref[...], preferred_element_type=F32)
              + jnp.dot(t_hi, rwl_ref[...], preferred_element_type=F32)) + rb_ref[...]
    lane = lax.broadcasted_iota(jnp.int32, logits.shape, 1).astype(F32)
    big = 1e9
    gl = jnp.where(lane < N_GROUPS, logits, -jnp.inf)
    gmax = jnp.max(gl, axis=-1, keepdims=True)
    gidx = jnp.min(jnp.where(gl == gmax, lane, big), axis=-1, keepdims=True)
    gw = 1.0 / jnp.sum(jnp.exp(gl - gmax), axis=-1, keepdims=True)
    lo = ROUTER_LANE0 + E_PER_GROUP * gidx
    el = jnp.where((lane >= lo) & (lane < lo + E_PER_GROUP), logits, -jnp.inf)
    v1 = jnp.max(el, axis=-1, keepdims=True)
    i1 = jnp.min(jnp.where(el == v1, lane, big), axis=-1, keepdims=True)
    el2 = jnp.where(lane == i1, -jnp.inf, el)
    v2 = jnp.max(el2, axis=-1, keepdims=True)
    i2 = jnp.min(jnp.where(el2 == v2, lane, big), axis=-1, keepdims=True)
    e2 = jnp.exp(v2 - v1)
    den = 1.0 + e2
    comb = jnp.where(lane == i1, (1.0 / den) * gw, 0.0) + jnp.where(lane == i2, (e2 / den) * gw, 0.0)
    combt_ref[...] = comb.T


def _outproj(merged, x, w_out, g, rwh, rwl, rb):
    n = x.shape[0]
    tm = _tile(n, (256, 128))
    full = lambda shape: pl.BlockSpec(shape, lambda i: (0,) * len(shape))
    row = lambda w: pl.BlockSpec((tm, w), lambda i: (i, 0))
    return pl.pallas_call(
        _outproj_kernel,
        grid=(n // tm,),
        in_specs=[row(D_MODEL), row(D_MODEL), full((D_MODEL, D_MODEL)), full((1, D_MODEL)),
                  full((D_MODEL, LANES)), full((D_MODEL, LANES)), full((1, LANES))],
        out_specs=[row(D_MODEL), row(D_MODEL), pl.BlockSpec((LANES, tm), lambda i: (0, i))],
        out_shape=[jax.ShapeDtypeStruct((n, D_MODEL), F32), jax.ShapeDtypeStruct((n, D_MODEL), BF16),
                   jax.ShapeDtypeStruct((LANES, n), F32)],
        compiler_params=_params("parallel"),
        name="outproj",
    )(merged, x, w_out, g, rwh, rwl, rb)


MOE_CAP = 96


def _moe_kernel(h_ref, ct_ref, x_ref, wg_ref, wu_ref, wd_ref, fg_ref, o_ref, acc_ref, rank_ref, *, final, th):
    e = pl.program_id(1)
    hh = pl.program_id(2)
    rows = pl.ds(pl.multiple_of(hh * th, th), th)

    @pl.when(e == 0)
    def _():
        acc_ref[rows, :] = jnp.zeros((th, D_MODEL), F32)
        sel = jnp.where(ct_ref[...] > 0.0, 1.0, 0.0).astype(BF16)
        before = (lax.broadcasted_iota(jnp.int32, (th, th), 0) < lax.broadcasted_iota(jnp.int32, (th, th), 1))
        rank_ref[hh] = jnp.dot(sel, jnp.where(before, 1.0, 0.0).astype(BF16), preferred_element_type=F32)

    row = ROUTER_LANE0 + e
    w_row = ct_ref[pl.ds(row, 1), :]
    r_row = rank_ref[hh, pl.ds(row, 1), :]
    sel_row = w_row > 0.0
    count = jnp.sum(jnp.where(sel_row, 1.0, 0.0), axis=1, keepdims=True)[0, 0].astype(jnp.int32)
    n_chunks = lax.div(count + (MOE_CAP - 1), MOE_CAP)

    def chunk(s, carry):
        slot = lax.broadcasted_iota(jnp.int32, (MOE_CAP, th), 0).astype(F32) + (s * MOE_CAP).astype(F32)
        pm = (slot == r_row) & sel_row
        pb = jnp.where(pm, 1.0, 0.0).astype(BF16)
        xc = jnp.dot(pb, h_ref[rows, :], preferred_element_type=F32).astype(BF16)
        wc = jnp.sum(jnp.where(pm, w_row, 0.0), axis=1, keepdims=True)
        gate = jnp.dot(xc, wg_ref[0], preferred_element_type=F32)
        up = jnp.dot(xc, wu_ref[0], preferred_element_type=F32)
        hid = (gate * jax.nn.sigmoid(gate) * up * wc).astype(BF16)
        y = jnp.dot(hid, wd_ref[0], preferred_element_type=F32)
        y_hi, y_lo = _split_bf16(y)
        back = lax.dot_general(jnp.concatenate([pb, pb], axis=0), jnp.concatenate([y_hi, y_lo], axis=0),
                               (((0,), (0,)), ((), ())), preferred_element_type=F32)
        acc_ref[rows, :] += back
        return carry

    lax.fori_loop(0, n_chunks, chunk, 0)

    @pl.when(e == pl.num_programs(1) - 1)
    def _():
        y = x_ref[...] + acc_ref[rows, :]
        o_ref[...] = _rms(y, fg_ref[...], EPS) if final else y


def _moe(h2, combt, x, wg, wu, wd, fg, final):
    n = x.shape[0]
    th = _tile(n, (512, 256))
    nh = 2 if n % (2 * th) == 0 else 1
    tb = nh * th
    last = N_EXPERTS - 1
    half = lambda i, e, hh: (nh * i + jnp.where(e == last, hh, 0), 0)
    return pl.pallas_call(
        functools.partial(_moe_kernel, final=final, th=th),
        grid=(n // tb, N_EXPERTS, nh),
        in_specs=[pl.BlockSpec((tb, D_MODEL), lambda i, e, hh: (i, 0)),
                  pl.BlockSpec((LANES, th), lambda i, e, hh: (0, nh * i + hh)),
                  pl.BlockSpec((th, D_MODEL), half),
                  pl.BlockSpec((1, D_MODEL, EXPERT_FF), lambda i, e, hh: (e, 0, 0)),
                  pl.BlockSpec((1, D_MODEL, EXPERT_FF), lambda i, e, hh: (e, 0, 0)),
                  pl.BlockSpec((1, EXPERT_FF, D_MODEL), lambda i, e, hh: (e, 0, 0)),
                  pl.BlockSpec((1, D_MODEL), lambda i, e, hh: (0, 0))],
        out_specs=pl.BlockSpec((th, D_MODEL), half),
        out_shape=jax.ShapeDtypeStruct((n, D_MODEL), F32),
        scratch_shapes=[pltpu.VMEM((tb, D_MODEL), F32), pltpu.VMEM((nh, LANES, th), F32)],
        compiler_params=_params("parallel", "arbitrary", "arbitrary"),
        name="moe",
    )(h2, combt, x, wg, wu, wd, fg)


def _layer_weights(l, norm_mix_g, w_in, mla_q_norm_g, mla_w_uq, mla_kv_norm_g, mla_w_uk, mla_w_uv,
                   conv_w, conv_b, conv_ln_g, conv_ln_b, pool_w, pool_scale,
                   diff_lq1, diff_lk1, diff_lq2, diff_lk2, diff_subln_g,
                   w_branch, w_out, norm_ffn_g, router_g_w, router_g_b, router_e_w, router_e_b,
                   moe_w_gate, moe_w_up, moe_w_down):
    wi = w_in[l]
    o = [0]
    for s in (Q_LORA, KV_LORA, MLA_ROPE, 2 * BRANCH, BRANCH, 512, 256, 256, GATE_COLS):
        o.append(o[-1] + s)
    sl = lambda k: wi[:, o[k]:o[k + 1]]
    glu = sl(3)
    w_mix = jnp.concatenate([sl(0), sl(1), sl(4), glu[:, :BRANCH], glu[:, BRANCH:], sl(5), sl(6), sl(7), sl(2),
                             jnp.zeros((D_MODEL, LANES - MLA_ROPE), F32)], axis=1).astype(BF16)
    uq = mla_w_uq[l]
    half = MLA_ROPE // 2
    wuq = jnp.concatenate([uq[:, :, :MLA_NOPE].reshape(Q_LORA, -1),
                           uq[:, :, MLA_NOPE:MLA_NOPE + half].reshape(Q_LORA, -1),
                           uq[:, :, MLA_NOPE + half:].reshape(Q_LORA, -1)], axis=1).astype(BF16)
    rw = jnp.concatenate([router_g_w[l], jnp.transpose(router_e_w[l], (1, 0, 2)).reshape(D_MODEL, N_EXPERTS),
                          jnp.zeros((D_MODEL, LANES - N_GROUPS - N_EXPERTS), F32)], axis=1)
    rb = jnp.concatenate([router_g_b[l], router_e_b[l].reshape(-1),
                          jnp.zeros((LANES - N_GROUPS - N_EXPERTS,), F32)])[None, :]
    rwh = rw.astype(BF16)
    rwl = (rw - rwh.astype(F32)).astype(BF16)
    return dict(
        norm_mix=norm_mix_g[l][None, :], w_mix=w_mix, w_gate=sl(8).astype(BF16),
        gq=mla_q_norm_g[l][None, :], gkv=mla_kv_norm_g[l][None, :], wuq=wuq,
        wuk=jnp.transpose(mla_w_uk[l], (1, 2, 0)).astype(BF16), wuv=jnp.transpose(mla_w_uv[l], (1, 0, 2)).astype(BF16),
        cw=conv_w[l], cb=conv_b[l][None, :], lg=conv_ln_g[l][None, :], lb=conv_ln_b[l][None, :],
        pw=pool_w[l].astype(BF16), ps=pool_scale[l][None, :],
        lam=(diff_lq1[l][None, :], diff_lk1[l][None, :], diff_lq2[l][None, :], diff_lk2[l][None, :]),
        sub=diff_subln_g[l][None, :],
        wb=w_branch[l].astype(BF16), w_out=w_out[l].astype(BF16), norm_ffn=norm_ffn_g[l][None, :],
        rwh=rwh, rwl=rwl, rb=rb,
        wg=moe_w_gate[l].astype(BF16), wu=moe_w_up[l].astype(BF16), wd=moe_w_down[l].astype(BF16))


def _pool_counts(pos):
    cols = [jnp.broadcast_to(jnp.minimum(w, pos + 1).astype(F32)[:, None], (pos.shape[0], POOL_GROUP))
            for w in POOL_WINDOWS]
    return jnp.concatenate(cols, axis=1)


def _tile_rows(tab, rows):
    return jnp.tile(tab, (rows // tab.shape[0], 1)) if tab.shape[0] < rows else tab


def _trunk(x, pos, b, t, prev_conv, prev_pool, attend, weights, final_g, depth):
    tab_rows = t if t >= 64 else 64 * t
    tabs = tuple(_tile_rows(tb, tab_rows) for tb in _rope_tables(pos))
    cnt = _pool_counts(pos)
    news = []
    for l in range(depth):
        p = weights[l]
        lam_init = 0.8 - 0.6 * math.exp(-0.3 * l)
        zmix, h = _inproj(x, p['norm_mix'], p['w_mix'])
        gates = _gates(h, p['w_gate'])
        qmla, rows, dqr, dkr = _prep(zmix, p['gq'], p['gkv'], p['wuq'], p['wuk'], tabs, attend.q_dtype)
        a_out, d_out = attend(l, p, qmla, rows, dqr, dkr, zmix, lam_init)
        pc = jnp.pad(prev_conv[l], ((0, 0), (CONV_HALO - CONV_PREV, 0), (0, 0)))
        pp = jnp.pad(prev_pool[l], ((0, 0), (POOL_HALO - POOL_PREV, 0), (0, 0)))
        b_out, c_out, new_conv, new_pool = _seq_mixers(zmix, pc, pp, p['cw'], p['cb'], p['lg'], p['lb'],
                                                       p['pw'], p['ps'], cnt, b, t)
        merged = _merge(gates, (a_out, b_out, c_out, d_out), p['wb'])
        x, h2, combt = _outproj(merged, x, p['w_out'], p['norm_ffn'], p['rwh'], p['rwl'], p['rb'])
        x = _moe(h2, combt, x, p['wg'], p['wu'], p['wd'], final_g, final=(l == depth - 1))
        news.append((rows.reshape(b, t, MLA_ROW), dkr.reshape(b, t, DIFF_KV, 2, DIFF_DH),
                     zmix[:, MIX_DV:MIX_DV + 256].reshape(b, t, DIFF_KV, DIFF_VD), new_conv, new_pool))
    stacked = [jnp.stack([nw[i] for nw in news], axis=0) for i in range(5)]
    return x.reshape(b, t, D_MODEL), stacked


class _PromptAttend:
    q_dtype = BF16

    def __init__(self, b, t):
        self.b, self.t = b, t

    def __call__(self, l, p, qmla, rows, dqr, dkr, zmix, lam_init):
        a_out = _mla_prompt(qmla, rows, p['wuv'], self.b, self.t)
        d_out = _diff_prompt(dqr, dkr, zmix, p['lam'], p['sub'], self.b, self.t, lam_init)
        return a_out, d_out


class _PagedAttend:
    q_dtype = F32

    def __init__(self, page_table, cache_mla, cache_k, cache_v, tn):
        depth, n_pool, ps = cache_mla.shape[:3]
        self.pt, self.tn = page_table, tn
        self.cm = jnp.transpose(cache_mla, (0, 1, 3, 2))
        self.ck = jnp.transpose(cache_k, (0, 1, 3, 4, 5, 2)).reshape(depth, n_pool, 2 * DIFF_KV * DIFF_DH, ps)
        self.cv = cache_v.reshape(depth, n_pool, ps * DIFF_KV, DIFF_VD)

    def __call__(self, l, p, qmla, rows, dqr, dkr, zmix, lam_init):
        a_out = _mla_paged(self.pt, qmla, rows, p['wuv'], self.cm, l, self.tn)
        d_out = _diff_paged(self.pt, dqr, dkr, zmix, p['lam'], p['sub'], self.ck, self.cv, l, self.tn, lam_init)
        return a_out, d_out


def kernel(x_prompt, x_sample, cache_mla_kv, cache_diff_k, cache_diff_v, state_conv, state_pool, page_table, norm_mix_g, w_in, mla_q_norm_g, mla_w_uq, mla_kv_norm_g, mla_w_uk, mla_w_uv, conv_w, conv_b, conv_ln_g, conv_ln_b, pool_w, pool_scale, diff_lq1, diff_lk1, diff_lq2, diff_lk2, diff_subln_g, w_branch, w_out, norm_ffn_g, router_g_w, router_g_b, router_e_w, router_e_b, moe_w_gate, moe_w_up, moe_w_down, final_norm_g):
    depth = w_in.shape[0]
    weights = [_layer_weights(l, norm_mix_g, w_in, mla_q_norm_g, mla_w_uq, mla_kv_norm_g, mla_w_uk, mla_w_uv,
                              conv_w, conv_b, conv_ln_g, conv_ln_b, pool_w, pool_scale,
                              diff_lq1, diff_lk1, diff_lq2, diff_lk2, diff_subln_g,
                              w_branch, w_out, norm_ffn_g, router_g_w, router_g_b, router_e_w, router_e_b,
                              moe_w_gate, moe_w_up, moe_w_down) for l in range(depth)]
    final_g = final_norm_g[None, :]

    bp, tp, _ = x_prompt.shape
    pos_p = jnp.arange(tp, dtype=jnp.int32)
    conv0 = jnp.zeros((depth, bp, CONV_PREV, BRANCH), F32)
    pool0 = jnp.zeros((depth, bp, POOL_PREV, BRANCH), F32)
    y_p, news_p = _trunk(x_prompt.reshape(bp * tp, D_MODEL), pos_p, bp, tp, conv0, pool0,
                         _PromptAttend(bp, tp), weights, final_g, depth)

    bs, ts, _ = x_sample.shape
    past_len = page_table.shape[1] * cache_mla_kv.shape[2]
    pos_s = past_len + jnp.arange(ts, dtype=jnp.int32)
    y_s, news_s = _trunk(x_sample.reshape(bs * ts, D_MODEL), pos_s, bs, ts, state_conv, state_pool,
                         _PagedAttend(page_table, cache_mla_kv, cache_diff_k, cache_diff_v, ts), weights, final_g, depth)

    return (y_p, y_s, *news_p, *news_s)
```

```python
import functools
import math

import jax
import jax.numpy as jnp
from jax import lax
from jax.experimental import pallas as pl
from jax.experimental.pallas import tpu as pltpu

F32 = jnp.float32
BF16 = jnp.bfloat16

D_MODEL = 2048
BRANCH = 512
N_BRANCH = 4
MLA_NOPE = 64
MLA_ROPE = 32
MLA_VH = 64
MLA_H = 8
Q_LORA = 384
KV_LORA = 128
MLA_ROW = KV_LORA + MLA_ROPE
MLA_THETA = 10000.0
LOG2E = math.log2(math.e)
MLA_SCALE = LOG2E / math.sqrt(MLA_NOPE + MLA_ROPE)
CONV_W = 31
CONV_PREV = CONV_W - 1
POOL_WINDOWS = (2, 4, 8, 16)
POOL_GROUP = 128
POOL_PREV = 15
DIFF_DH = 64
DIFF_VD = 128
DIFF_KV = 2
DIFF_REP = 2
DIFF_SCALE = LOG2E / math.sqrt(DIFF_DH)
ROT_DIM = 16
ROPE_THETA = 500000.0
N_GROUPS = 4
E_PER_GROUP = 4
N_EXPERTS = 16
EXPERT_FF = 512
EPS = 1e-6
NEG_INF = -1e30
LANES = 128

MIX_QC = 0
MIX_KVC = 384
MIX_POOL = 512
MIX_GLU_A = 1024
MIX_GLU_G = 1536
MIX_DQ = 2048
MIX_DK = 2560
MIX_DV = 2816
MIX_KROPE = 3072
MIX_COLS = 3200
GATE_COLS = N_BRANCH * D_MODEL
ROUTER_LANE0 = N_GROUPS

VMEM_LIMIT = 56 * 1024 * 1024


def _params(*sem):
    return pltpu.CompilerParams(dimension_semantics=sem, vmem_limit_bytes=VMEM_LIMIT)


def _tile(n, cands):
    for c in cands:
        if n % c == 0:
            return c
    raise ValueError(f"no tile for {n} in {cands}")


def _rms(x, g, eps):
    return x * lax.rsqrt(jnp.mean(x * x, axis=-1, keepdims=True) + eps) * g


def _inproj_kernel(x_ref, g_ref, w_ref, z_ref, h_ref):
    @pl.when(pl.program_id(1) == 0)
    def _():
        h_ref[...] = _rms(x_ref[...], g_ref[...], EPS).astype(BF16)

    z_ref[...] = jnp.dot(h_ref[...], w_ref[...], preferred_element_type=F32)


def _inproj(x, g, w_mix):
    n = x.shape[0]
    tm = _tile(n, (512, 256))
    tn = 640
    return pl.pallas_call(
        _inproj_kernel,
        grid=(n // tm, MIX_COLS // tn),
        in_specs=[pl.BlockSpec((tm, D_MODEL), lambda i, j: (i, 0)),
                  pl.BlockSpec((1, D_MODEL), lambda i, j: (0, 0)),
                  pl.BlockSpec((D_MODEL, tn), lambda i, j: (0, j))],
        out_specs=[pl.BlockSpec((tm, tn), lambda i, j: (i, j)),
                   pl.BlockSpec((tm, D_MODEL), lambda i, j: (i, 0))],
        out_shape=[jax.ShapeDtypeStruct((n, MIX_COLS), F32),
                   jax.ShapeDtypeStruct((n, D_MODEL), BF16)],
        compiler_params=_params("parallel", "arbitrary"),
        name="inproj",
    )(x, g, w_mix)


def _gates_kernel(h_ref, w_ref, o_ref):
    o_ref[...] = jax.nn.sigmoid(jnp.dot(h_ref[...], w_ref[...], preferred_element_type=F32))


def _gates(h, w_gate):
    n = h.shape[0]
    tm = _tile(n, (1024, 512, 256))
    tn = 1024
    return pl.pallas_call(
        _gates_kernel,
        grid=(n // tm, GATE_COLS // tn),
        in_specs=[pl.BlockSpec((tm, D_MODEL), lambda i, j: (i, 0)),
                  pl.BlockSpec((D_MODEL, tn), lambda i, j: (0, j))],
        out_specs=pl.BlockSpec((tm, tn), lambda i, j: (i, j)),
        out_shape=jax.ShapeDtypeStruct((n, GATE_COLS), F32),
        compiler_params=_params("parallel", "arbitrary"),
        name="gates",
    )(h, w_gate)


def _prep_kernel(qc_ref, kvc_ref, kr_ref, dq_ref, dk_ref, gq_ref, gkv_ref, wuq_ref, wuk_ref,
                 c8_ref, s8_ref, c16_ref, s16_ref, dc_ref, dlo_ref, dhi_ref,
                 qmla_ref, rows_ref, dqr_ref, dkr_ref):
    qn = _rms(qc_ref[...], gq_ref[...], EPS).astype(BF16)
    q = jnp.dot(qn, wuq_ref[...], preferred_element_type=F32)
    nope = q[:, :MLA_H * MLA_NOPE].astype(BF16)
    r1 = q[:, 512:640]
    r2 = q[:, 640:768]
    c8, s8 = c8_ref[...], s8_ref[...]
    pe1 = r1 * c8 - r2 * s8
    pe2 = r1 * s8 + r2 * c8
    half = MLA_ROPE // 2
    for h in range(MLA_H):
        lat = jnp.dot(nope[:, h * MLA_NOPE:(h + 1) * MLA_NOPE], wuk_ref[h], preferred_element_type=F32)
        qh = jnp.concatenate([lat, pe1[:, h * half:(h + 1) * half], pe2[:, h * half:(h + 1) * half]], axis=-1)
        qmla_ref[h] = (qh * MLA_SCALE).astype(qmla_ref.dtype)
    ckv = _rms(kvc_ref[...], gkv_ref[...], EPS)
    kr = kr_ref[...]
    k1 = kr[:, :half]
    k2 = kr[:, half:MLA_ROPE]
    c16, s16 = c16_ref[...], s16_ref[...]
    rows_ref[...] = jnp.concatenate([ckv, k1 * c16 - k2 * s16, k1 * s16 + k2 * c16], axis=-1)
    dc, dlo, dhi = dc_ref[...], dlo_ref[...], dhi_ref[...]

    def rot(x):
        return x * dc + pltpu.roll(x, LANES - ROT_DIM // 2, 1) * dlo + pltpu.roll(x, ROT_DIM // 2, 1) * dhi

    for c in range(4):
        sl = slice(c * LANES, (c + 1) * LANES)
        dqr_ref[:, sl] = (rot(dq_ref[:, sl]) * DIFF_SCALE).astype(dqr_ref.dtype)
    for c in range(2):
        sl = slice(c * LANES, (c + 1) * LANES)
        dkr_ref[:, sl] = rot(dk_ref[:, sl])


def _prep(zmix, gq, gkv, wuq, wuk, tabs, q_dtype):
    n = zmix.shape[0]
    tab_rows = tabs[0].shape[0]
    tm = _tile(math.gcd(n, tab_rows), (512, 256, 128, 64))
    nt = tab_rows // tm
    row = lambda w, c: pl.BlockSpec((tm, w), lambda i: (i, c))
    tab = lambda w: pl.BlockSpec((tm, w), lambda i: (i % nt, 0))
    full = lambda shape: pl.BlockSpec(shape, lambda i: (0,) * len(shape))
    return pl.pallas_call(
        _prep_kernel,
        grid=(n // tm,),
        in_specs=[row(Q_LORA, MIX_QC // Q_LORA), row(KV_LORA, MIX_KVC // KV_LORA), row(LANES, MIX_KROPE // LANES),
                  row(512, MIX_DQ // 512), row(256, MIX_DK // 256),
                  full((1, Q_LORA)), full((1, KV_LORA)), full((Q_LORA, 768)), full((MLA_H, MLA_NOPE, KV_LORA)),
                  tab(LANES), tab(LANES), tab(16), tab(16), tab(LANES), tab(LANES), tab(LANES)],
        out_specs=[pl.BlockSpec((MLA_H, tm, MLA_ROW), lambda i: (0, i, 0)),
                   pl.BlockSpec((tm, MLA_ROW), lambda i: (i, 0)),
                   pl.BlockSpec((tm, 512), lambda i: (i, 0)),
                   pl.BlockSpec((tm, 256), lambda i: (i, 0))],
        out_shape=[jax.ShapeDtypeStruct((MLA_H, n, MLA_ROW), q_dtype),
                   jax.ShapeDtypeStruct((n, MLA_ROW), F32),
                   jax.ShapeDtypeStruct((n, 512), q_dtype),
                   jax.ShapeDtypeStruct((n, 256), F32)],
        compiler_params=_params("parallel"),
        name="prep",
    )(zmix, zmix, zmix, zmix, zmix, gq, gkv, wuq, wuk, *tabs)


def _rope_tables(pos):
    posf = pos.astype(F32)[:, None]
    inv_m = 1.0 / (MLA_THETA ** (jnp.arange(0, MLA_ROPE, 2, dtype=F32) / MLA_ROPE))
    ang_m = posf * inv_m[None, :]
    c16, s16 = jnp.cos(ang_m), jnp.sin(ang_m)
    c8, s8 = jnp.tile(c16, (1, MLA_H)), jnp.tile(s16, (1, MLA_H))
    inv_d = 1.0 / (ROPE_THETA ** (jnp.arange(0, ROT_DIM, 2, dtype=F32) / ROT_DIM))
    ang_d = posf * inv_d[None, :]
    cd, sd = jnp.cos(ang_d), jnp.sin(ang_d)
    t = pos.shape[0]
    hr = ROT_DIM // 2
    one = jnp.ones((t, DIFF_DH - ROT_DIM), F32)
    zero = jnp.zeros((t, DIFF_DH - ROT_DIM), F32)
    z8 = jnp.zeros((t, hr), F32)
    dc = jnp.tile(jnp.concatenate([cd, cd, one], axis=1), (1, 2))
    dlo = jnp.tile(jnp.concatenate([-sd, z8, zero], axis=1), (1, 2))
    dhi = jnp.tile(jnp.concatenate([z8, sd, zero], axis=1), (1, 2))
    return c8, s8, c16, s16, dc, dlo, dhi


def _nt_dot(a, b):
    return lax.dot_general(a, b, (((1,), (1,)), ((), ())), preferred_element_type=F32)


def _init_state(m_ref, l_ref, acc_ref):
    m_ref[...] = jnp.full(m_ref.shape, NEG_INF, F32)
    l_ref[...] = jnp.zeros(l_ref.shape, F32)
    acc_ref[...] = jnp.zeros(acc_ref.shape, F32)


def _probs(s, m_ref, l_ref):
    m_prev = m_ref[...]
    m_new = jnp.maximum(m_prev, jnp.max(s, axis=1, keepdims=True))
    corr = jnp.exp2(m_prev - m_new)
    ps = [jnp.exp2(s[:, c:c + LANES] - m_new) for c in range(0, s.shape[1], LANES)]
    lsum = ps[0]
    for p in ps[1:]:
        lsum = lsum + p
    l_ref[...] = l_ref[...] * corr + lsum
    m_ref[...] = m_new
    p = ps[0] if len(ps) == 1 else jnp.concatenate(ps, axis=1)
    return p.astype(BF16), corr


def _causal(s, tq):
    t = lax.broadcasted_iota(jnp.int32, s.shape, 0) & (tq - 1)
    c = lax.broadcasted_iota(jnp.int32, s.shape, 1)
    return jnp.where(c <= t, s, NEG_INF)


def _tri_pairs(nq):
    qi = [q for q in range(nq) for _ in range(q + 1)]
    ki = [k for q in range(nq) for k in range(q + 1)]
    return jnp.asarray(qi, jnp.int32), jnp.asarray(ki, jnp.int32)


def _mla_prompt_kernel(qi_ref, ki_ref, q_ref, k_ref, wuv_ref, o_ref, m_ref, l_ref, acc_ref, *, tq):
    pair = pl.program_id(1)
    qi, ki = qi_ref[pair], ki_ref[pair]

    @pl.when(ki == 0)
    def _():
        _init_state(m_ref, l_ref, acc_ref)

    def step(masked):
        k = k_ref[...].astype(BF16)
        v = k[:, :KV_LORA]
        for h in range(MLA_H):
            rs = slice(h * tq, (h + 1) * tq)
            s = _nt_dot(q_ref[h], k)
            if masked:
                s = _causal(s, tq)
            p, corr = _probs(s, m_ref.at[rs, :], l_ref.at[rs, :])
            acc_ref[rs, :] = acc_ref[rs, :] * corr + jnp.dot(p, v, preferred_element_type=F32)

    @pl.when(ki < qi)
    def _():
        step(False)

    @pl.when(ki == qi)
    def _():
        step(True)
        outs = []
        for h in range(MLA_H):
            rs = slice(h * tq, (h + 1) * tq)
            o = acc_ref[rs, :] / jnp.sum(l_ref[rs, :], axis=1, keepdims=True)
            outs.append(jnp.dot(o.astype(BF16), wuv_ref[h], preferred_element_type=F32))
        o_ref[...] = jnp.concatenate(outs, axis=-1)


def _mla_prompt(qmla, rows, wuv, b, t):
    tq = _tile(t, (256, 128))
    assert tq & (tq - 1) == 0
    nq = t // tq
    qi_tab, ki_tab = _tri_pairs(nq)
    grid_spec = pltpu.PrefetchScalarGridSpec(
        num_scalar_prefetch=2,
        grid=(b, qi_tab.shape[0]),
        in_specs=[pl.BlockSpec((MLA_H, tq, MLA_ROW), lambda bi, p, qt, kt: (0, bi * nq + qt[p], 0)),
                  pl.BlockSpec((tq, MLA_ROW), lambda bi, p, qt, kt: (bi * nq + kt[p], 0)),
                  pl.BlockSpec((MLA_H, KV_LORA, MLA_VH), lambda bi, p, qt, kt: (0, 0, 0))],
        out_specs=pl.BlockSpec((tq, BRANCH), lambda bi, p, qt, kt: (bi * nq + qt[p], 0)),
        scratch_shapes=[pltpu.VMEM((MLA_H * tq, LANES), F32), pltpu.VMEM((MLA_H * tq, LANES), F32),
                        pltpu.VMEM((MLA_H * tq, KV_LORA), F32)],
    )
    return pl.pallas_call(
        functools.partial(_mla_prompt_kernel, tq=tq),
        grid_spec=grid_spec,
        out_shape=jax.ShapeDtypeStruct((b * t, BRANCH), F32),
        compiler_params=_params("parallel", "arbitrary"),
        name="mla_prompt",
    )(qi_tab, ki_tab, qmla, rows, wuv)


def _lambda(lq1_ref, lk1_ref, lq2_ref, lk2_ref, lam_init):
    a = jnp.exp(jnp.sum(lq1_ref[...] * lk1_ref[...], axis=-1, keepdims=True))
    b = jnp.exp(jnp.sum(lq2_ref[...] * lk2_ref[...], axis=-1, keepdims=True))
    return a - b + lam_init


def _diff_prompt_kernel(qi_ref, ki_ref, q_ref, k_ref, v_ref, lq1_ref, lk1_ref, lq2_ref, lk2_ref, sub_ref, o_ref,
                        qs_ref, m_ref, l_ref, acc_ref, *, tq, lam_init):
    pair = pl.program_id(1)
    qi, ki = qi_ref[pair], ki_ref[pair]
    rr = DIFF_REP * tq

    @pl.when(ki == 0)
    def _():
        _init_state(m_ref, l_ref, acc_ref)
        q = q_ref[...]
        for g in range(DIFF_KV):
            for m in range(2):
                c = g * 2 + m
                for r in range(DIFF_REP):
                    col = g * 256 + r * 128 + m * DIFF_DH
                    qs_ref[c * rr + r * tq:c * rr + (r + 1) * tq, :] = q[:, col:col + DIFF_DH]

    def step(masked):
        k = k_ref[...].astype(BF16)
        v = v_ref[...].astype(BF16)
        for c in range(4):
            g = c // 2
            rs = slice(c * rr, (c + 1) * rr)
            s = _nt_dot(qs_ref[rs, :], k[:, c * DIFF_DH:(c + 1) * DIFF_DH])
            if masked:
                s = _causal(s, tq)
            p, corr = _probs(s, m_ref.at[rs, :], l_ref.at[rs, :])
            acc_ref[rs, :] = acc_ref[rs, :] * corr + jnp.dot(p, v[:, g * DIFF_VD:(g + 1) * DIFF_VD],
                                                             preferred_element_type=F32)

    @pl.when(ki < qi)
    def _():
        step(False)

    @pl.when(ki == qi)
    def _():
        step(True)
        lam = _lambda(lq1_ref, lk1_ref, lq2_ref, lk2_ref, lam_init)
        o = acc_ref[...] / jnp.sum(l_ref[...], axis=1, keepdims=True)
        for g in range(DIFF_KV):
            for r in range(DIFF_REP):
                o0 = o[(2 * g) * rr + r * tq:(2 * g) * rr + (r + 1) * tq]
                o1 = o[(2 * g + 1) * rr + r * tq:(2 * g + 1) * rr + (r + 1) * tq]
                d = _rms(o0 - lam * o1, sub_ref[...], 1e-5) * (1.0 - lam_init)
                col = (g * DIFF_REP + r) * DIFF_VD
                o_ref[:, col:col + DIFF_VD] = d


def _diff_prompt(dqr, dkr, zmix, lam_p, sub, b, t, lam_init):
    tq = _tile(t, (256, 128))
    assert tq & (tq - 1) == 0
    nq = t // tq
    qi_tab, ki_tab = _tri_pairs(nq)
    vec = lambda w: pl.BlockSpec((1, w), lambda bi, p, qt, kt: (0, 0))
    nrow = 4 * DIFF_REP * tq
    grid_spec = pltpu.PrefetchScalarGridSpec(
        num_scalar_prefetch=2,
        grid=(b, qi_tab.shape[0]),
        in_specs=[pl.BlockSpec((tq, 512), lambda bi, p, qt, kt: (bi * nq + qt[p], 0)),
                  pl.BlockSpec((tq, 256), lambda bi, p, qt, kt: (bi * nq + kt[p], 0)),
                  pl.BlockSpec((tq, 256), lambda bi, p, qt, kt: (bi * nq + kt[p], MIX_DV // 256)),
                  vec(DIFF_DH), vec(DIFF_DH), vec(DIFF_DH), vec(DIFF_DH), vec(DIFF_VD)],
        out_specs=pl.BlockSpec((tq, BRANCH), lambda bi, p, qt, kt: (bi * nq + qt[p], 0)),
        scratch_shapes=[pltpu.VMEM((nrow, DIFF_DH), BF16), pltpu.VMEM((nrow, LANES), F32),
                        pltpu.VMEM((nrow, LANES), F32), pltpu.VMEM((nrow, DIFF_VD), F32)],
    )
    return pl.pallas_call(
        functools.partial(_diff_prompt_kernel, tq=tq, lam_init=lam_init),
        grid_spec=grid_spec,
        out_shape=jax.ShapeDtypeStruct((b * t, BRANCH), F32),
        compiler_params=_params("parallel", "arbitrary"),
        name="diff_prompt",
    )(qi_tab, ki_tab, dqr, dkr, zmix, *lam_p, sub)


def _pad_new(x):
    pad = jnp.zeros((LANES - x.shape[0], x.shape[1]), x.dtype)
    return jnp.concatenate([x, pad], axis=0).astype(BF16)


def _page_pipeline(pt_ref, streams, npg, layer):
    step = pl.program_id(0) * pl.num_programs(1) + pl.program_id(1)
    total = pl.num_programs(0) * pl.num_programs(1)
    slot = step & 1

    def copies(st, sl):
        return [pltpu.make_async_copy(cache.at[layer, pt_ref[st * npg + i]], buf.at[sl, i], sem.at[sl])
                for cache, buf, sem in streams for i in range(npg)]

    @pl.when(step == 0)
    def _():
        for c in copies(0, 0):
            c.start()

    @pl.when(step + 1 < total)
    def _():
        for c in copies(step + 1, 1 - slot):
            c.start()

    for c in copies(step, slot):
        c.wait()
    return slot


def _mla_paged_kernel(pt_ref, q_ref, knew_ref, wuv_ref, cache_ref, o_ref, buf_ref, sem_ref, m_ref, l_ref, acc_ref,
                      *, npg, tn, layer):
    j = pl.program_id(1)
    slot = _page_pipeline(pt_ref, ((cache_ref, buf_ref, sem_ref),), npg, layer)

    @pl.when(j == 0)
    def _():
        _init_state(m_ref, l_ref, acc_ref)

    q = q_ref[...].reshape(MLA_H * tn, MLA_ROW).astype(BF16)
    kt = jnp.concatenate([buf_ref[slot, i] for i in range(npg)], axis=1).astype(BF16)
    p, corr = _probs(jnp.dot(q, kt, preferred_element_type=F32), m_ref, l_ref)
    acc_ref[...] = acc_ref[...] * corr + _nt_dot(p, kt[:KV_LORA, :])

    @pl.when(j == pl.num_programs(1) - 1)
    def _():
        kn = _pad_new(knew_ref[...])
        pn, cn = _probs(_causal(_nt_dot(q, kn), tn), m_ref, l_ref)
        acc = acc_ref[...] * cn + jnp.dot(pn, kn[:, :KV_LORA], preferred_element_type=F32)
        o = acc / jnp.sum(l_ref[...], axis=1, keepdims=True)
        outs = [jnp.dot(o[h * tn:(h + 1) * tn].astype(BF16), wuv_ref[h], preferred_element_type=F32)
                for h in range(MLA_H)]
        o_ref[...] = jnp.concatenate(outs, axis=-1)


def _mla_paged(page_table, qmla, rows, wuv, cache_t, layer, tn):
    nb, n_pages = page_table.shape
    ps = cache_t.shape[3]
    npg = _tile(n_pages, (64, 32, 16, 8, 4, 2, 1))
    assert tn & (tn - 1) == 0

    grid_spec = pltpu.PrefetchScalarGridSpec(
        num_scalar_prefetch=1,
        grid=(nb, n_pages // npg),
        in_specs=[pl.BlockSpec((MLA_H, tn, MLA_ROW), lambda b, j, pt: (0, b, 0)),
                  pl.BlockSpec((tn, MLA_ROW), lambda b, j, pt: (b, 0)),
                  pl.BlockSpec((MLA_H, KV_LORA, MLA_VH), lambda b, j, pt: (0, 0, 0)),
                  pl.BlockSpec(memory_space=pl.ANY)],
        out_specs=pl.BlockSpec((tn, BRANCH), lambda b, j, pt: (b, 0)),
        scratch_shapes=[pltpu.VMEM((2, npg, MLA_ROW, ps), F32), pltpu.SemaphoreType.DMA((2,)),
                        pltpu.VMEM((MLA_H * tn, LANES), F32), pltpu.VMEM((MLA_H * tn, LANES), F32),
                        pltpu.VMEM((MLA_H * tn, KV_LORA), F32)],
    )
    return pl.pallas_call(
        functools.partial(_mla_paged_kernel, npg=npg, tn=tn, layer=layer),
        grid_spec=grid_spec,
        out_shape=jax.ShapeDtypeStruct((nb * tn, BRANCH), F32),
        compiler_params=_params("arbitrary", "arbitrary"),
        name="mla_paged",
    )(page_table.reshape(-1), qmla, rows, wuv, cache_t)


def _diff_paged_kernel(pt_ref, q_ref, knew_ref, vnew_ref, lq1_ref, lk1_ref, lq2_ref, lk2_ref, sub_ref,
                       ck_ref, cv_ref, o_ref, kbuf_ref, vbuf_ref, ksem_ref, vsem_ref, qbd_ref, m_ref, l_ref, acc_ref,
                       *, npg, tn, ps, layer, lam_init):
    j = pl.program_id(1)
    rr = DIFF_REP * tn
    gr = 2 * rr
    slot = _page_pipeline(pt_ref, ((ck_ref, kbuf_ref, ksem_ref), (cv_ref, vbuf_ref, vsem_ref)), npg, layer)

    @pl.when(j == 0)
    def _():
        _init_state(m_ref, l_ref, acc_ref)
        qbd_ref[...] = jnp.zeros(qbd_ref.shape, F32)
        q = q_ref[...]
        for g in range(DIFF_KV):
            for m in range(2):
                c = g * 2 + m
                for r in range(DIFF_REP):
                    col = g * 256 + r * 128 + m * DIFF_DH
                    qbd_ref[c * rr + r * tn:c * rr + (r + 1) * tn, c * DIFF_DH:(c + 1) * DIFF_DH] = q[:, col:col + DIFF_DH]

    qbd = qbd_ref[...].astype(BF16)
    kt = jnp.concatenate([kbuf_ref[slot, i] for i in range(npg)], axis=1).astype(BF16)
    p, corr = _probs(jnp.dot(qbd, kt, preferred_element_type=F32), m_ref, l_ref)
    pv = []
    for g in range(DIFF_KV):
        vg = jnp.concatenate([vbuf_ref[slot, i, pl.ds(g, ps, stride=DIFF_KV), :] for i in range(npg)],
                             axis=0).astype(BF16)
        pv.append(jnp.dot(p[g * gr:(g + 1) * gr], vg, preferred_element_type=F32))
    acc_ref[...] = acc_ref[...] * corr + jnp.concatenate(pv, axis=0)

    @pl.when(j == pl.num_programs(1) - 1)
    def _():
        kn = _pad_new(knew_ref[...])
        vn = _pad_new(vnew_ref[...])
        pn, cn = _probs(_causal(_nt_dot(qbd, kn), tn), m_ref, l_ref)
        pvn = [jnp.dot(pn[g * gr:(g + 1) * gr], vn[:, g * DIFF_VD:(g + 1) * DIFF_VD], preferred_element_type=F32)
               for g in range(DIFF_KV)]
        acc = acc_ref[...] * cn + jnp.concatenate(pvn, axis=0)
        lam = _lambda(lq1_ref, lk1_ref, lq2_ref, lk2_ref, lam_init)
        o = acc / jnp.sum(l_ref[...], axis=1, keepdims=True)
        for g in range(DIFF_KV):
            for r in range(DIFF_REP):
                o0 = o[(2 * g) * rr + r * tn:(2 * g) * rr + (r + 1) * tn]
                o1 = o[(2 * g + 1) * rr + r * tn:(2 * g + 1) * rr + (r + 1) * tn]
                d = _rms(o0 - lam * o1, sub_ref[...], 1e-5) * (1.0 - lam_init)
                col = (g * DIFF_REP + r) * DIFF_VD
                o_ref[:, col:col + DIFF_VD] = d


def _diff_paged(page_table, dqr, dkr, zmix, lam_p, sub, cache_kt, cache_v2, layer, tn, lam_init):
    nb, n_pages = page_table.shape
    ps = cache_kt.shape[3]
    npg = _tile(n_pages, (32, 16, 8, 4, 2, 1))
    assert tn & (tn - 1) == 0
    vec = lambda w: pl.BlockSpec((1, w), lambda b, j, pt: (0, 0))
    nrow = 4 * DIFF_REP * tn
    grid_spec = pltpu.PrefetchScalarGridSpec(
        num_scalar_prefetch=1,
        grid=(nb, n_pages // npg),
        in_specs=[pl.BlockSpec((tn, 512), lambda b, j, pt: (b, 0)),
                  pl.BlockSpec((tn, 256), lambda b, j, pt: (b, 0)),
                  pl.BlockSpec((tn, 256), lambda b, j, pt: (b, MIX_DV // 256)),
                  vec(DIFF_DH), vec(DIFF_DH), vec(DIFF_DH), vec(DIFF_DH), vec(DIFF_VD),
                  pl.BlockSpec(memory_space=pl.ANY), pl.BlockSpec(memory_space=pl.ANY)],
        out_specs=pl.BlockSpec((tn, BRANCH), lambda b, j, pt: (b, 0)),
        scratch_shapes=[pltpu.VMEM((2, npg, 256, ps), F32), pltpu.VMEM((2, npg, DIFF_KV * ps, DIFF_VD), F32),
                        pltpu.SemaphoreType.DMA((2,)), pltpu.SemaphoreType.DMA((2,)),
                        pltpu.VMEM((nrow, 256), F32), pltpu.VMEM((nrow, LANES), F32), pltpu.VMEM((nrow, LANES), F32),
                        pltpu.VMEM((nrow, DIFF_VD), F32)],
    )
    return pl.pallas_call(
        functools.partial(_diff_paged_kernel, npg=npg, tn=tn, ps=ps, layer=layer, lam_init=lam_init),
        grid_spec=grid_spec,
        out_shape=jax.ShapeDtypeStruct((nb * tn, BRANCH), F32),
        compiler_params=_params("arbitrary", "arbitrary"),
        name="diff_paged",
    )(page_table.reshape(-1), dqr, dkr, zmix, *lam_p, sub, cache_kt, cache_v2)


CONV_HALO = 32
POOL_HALO = 16


def _seq_kernel(pool_ref, ga_ref, gg_ref, pc_ref, pp_ref, cw_ref, cb_ref, lg_ref, lb_ref, pw_ref, ps_ref, cnt_ref,
                bout_ref, cout_ref, nconv_ref, npool_ref, cwin_ref, pwin_ref, conv_ref, *, tc):
    i = pl.program_id(1)

    @pl.when(i == 0)
    def _():
        cwin_ref[0:CONV_HALO, :] = pc_ref[0]
        pwin_ref[0:POOL_HALO, :] = pp_ref[0]

    cwin_ref[CONV_HALO:CONV_HALO + tc, :] = ga_ref[...] * jax.nn.sigmoid(gg_ref[...])
    x = pool_ref[...]
    pwin_ref[POOL_HALO:POOL_HALO + tc, :] = x

    rc = min(tc, 64)
    off = CONV_HALO - CONV_PREV
    for r0 in range(0, tc, rc):
        for c0 in range(0, BRANCH, LANES):
            acc = jnp.broadcast_to(cb_ref[:, c0:c0 + LANES], (rc, LANES))
            for w in range(CONV_W):
                acc = acc + cwin_ref[off + r0 + w:off + r0 + w + rc, c0:c0 + LANES] * cw_ref[w:w + 1, c0:c0 + LANES]
            conv_ref[r0:r0 + rc, c0:c0 + LANES] = acc
    cv = conv_ref[...]
    mu = jnp.mean(cv, axis=-1, keepdims=True)
    var = jnp.mean(jnp.square(cv - mu), axis=-1, keepdims=True)
    y = (cv - mu) * lax.rsqrt(var + 1e-5) * lg_ref[...] + lb_ref[...]
    bout_ref[...] = y * jax.nn.sigmoid(y)

    for g, w in enumerate(POOL_WINDOWS):
        cs = slice(g * POOL_GROUP, (g + 1) * POOL_GROUP)
        s = pwin_ref[POOL_HALO:POOL_HALO + tc, cs]
        for jj in range(1, w):
            s = s + pwin_ref[POOL_HALO - jj:POOL_HALO - jj + tc, cs]
        pooled = (s / cnt_ref[:, cs] - x[:, cs]).astype(BF16)
        cout_ref[:, cs] = jnp.dot(pooled, pw_ref[g], preferred_element_type=F32) * ps_ref[:, cs]

    @pl.when(i == pl.num_programs(1) - 1)
    def _():
        nconv_ref[0] = cwin_ref[tc + off:tc + CONV_HALO, :]
        npool_ref[0] = pwin_ref[tc + 1:tc + POOL_HALO, :]

    carry_c = cwin_ref[tc:tc + CONV_HALO, :]
    carry_p = pwin_ref[tc:tc + POOL_HALO, :]
    cwin_ref[0:CONV_HALO, :] = carry_c
    pwin_ref[0:POOL_HALO, :] = carry_p


def _seq_mixers(zmix, prev_conv, prev_pool, cw, cb, lg, lb, pw, ps, cnt, b, t):
    tc = _tile(t, (128, 64, 32, 16, 8))
    nt = t // tc
    col = lambda c: pl.BlockSpec((tc, BRANCH), lambda bi, i: (bi * nt + i, c))
    full = lambda shape: pl.BlockSpec(shape, lambda bi, i: (0,) * len(shape))
    return pl.pallas_call(
        functools.partial(_seq_kernel, tc=tc),
        grid=(b, nt),
        in_specs=[col(MIX_POOL // BRANCH), col(MIX_GLU_A // BRANCH), col(MIX_GLU_G // BRANCH),
                  pl.BlockSpec((1, CONV_HALO, BRANCH), lambda bi, i: (bi, 0, 0)),
                  pl.BlockSpec((1, POOL_HALO, BRANCH), lambda bi, i: (bi, 0, 0)),
                  full((CONV_W, BRANCH)), full((1, BRANCH)), full((1, BRANCH)), full((1, BRANCH)),
                  full((len(POOL_WINDOWS), POOL_GROUP, POOL_GROUP)), full((1, BRANCH)),
                  pl.BlockSpec((tc, BRANCH), lambda bi, i: (i, 0))],
        out_specs=[pl.BlockSpec((tc, BRANCH), lambda bi, i: (bi * nt + i, 0)),
                   pl.BlockSpec((tc, BRANCH), lambda bi, i: (bi * nt + i, 0)),
                   pl.BlockSpec((1, CONV_PREV, BRANCH), lambda bi, i: (bi, 0, 0)),
                   pl.BlockSpec((1, POOL_PREV, BRANCH), lambda bi, i: (bi, 0, 0))],
        out_shape=[jax.ShapeDtypeStruct((b * t, BRANCH), F32), jax.ShapeDtypeStruct((b * t, BRANCH), F32),
                   jax.ShapeDtypeStruct((b, CONV_PREV, BRANCH), F32), jax.ShapeDtypeStruct((b, POOL_PREV, BRANCH), F32)],
        scratch_shapes=[pltpu.VMEM((CONV_HALO + tc, BRANCH), F32), pltpu.VMEM((POOL_HALO + tc, BRANCH), F32),
                        pltpu.VMEM((tc, BRANCH), F32)],
        compiler_params=_params("parallel", "arbitrary"),
        name="seq_mixers",
    )(zmix, zmix, zmix, prev_conv, prev_pool, cw, cb, lg, lb, pw, ps, cnt)


def _merge_kernel(g0, g1, g2, g3, a_ref, b_ref, c_ref, d_ref, wb_ref, o_ref):
    acc = None
    for n, (g, br) in enumerate(((g0, a_ref), (g1, b_ref), (g2, c_ref), (g3, d_ref))):
        up = jnp.dot(br[...].astype(BF16), wb_ref[n], preferred_element_type=F32)
        term = g[...] * up
        acc = term if acc is None else acc + term
    o_ref[...] = acc.astype(BF16)


def _merge(gates, branches, wb):
    n = gates.shape[0]
    tm = _tile(n, (512, 256))
    tn = 512
    nj = D_MODEL // tn
    gate = lambda k: pl.BlockSpec((tm, tn), lambda i, j: (i, k * nj + j))
    br = pl.BlockSpec((tm, BRANCH), lambda i, j: (i, 0))
    return pl.pallas_call(
        _merge_kernel,
        grid=(n // tm, nj),
        in_specs=[gate(0), gate(1), gate(2), gate(3), br, br, br, br,
                  pl.BlockSpec((N_BRANCH, BRANCH, tn), lambda i, j: (0, 0, j))],
        out_specs=pl.BlockSpec((tm, tn), lambda i, j: (i, j)),
        out_shape=jax.ShapeDtypeStruct((n, D_MODEL), BF16),
        compiler_params=_params("parallel", "arbitrary"),
        name="merge",
    )(gates, gates, gates, gates, *branches, wb)


def _split_bf16(x):
    hi = x.astype(BF16)
    return hi, (x - hi.astype(F32)).astype(BF16)


def _outproj_kernel(m_ref, x_ref, w_ref, g_ref, rwh_ref, rwl_ref, rb_ref, xo_ref, h_ref, combt_ref):
    xn = x_ref[...] + jnp.dot(m_ref[...], w_ref[...], preferred_element_type=F32)
    xo_ref[...] = xn
    t = _rms(xn, g_ref[...], EPS)
    h_ref[...] = t.astype(BF16)
    t_hi, t_lo = _split_bf16(t)
    logits = (jnp.dot(t_hi, rwh_ref[...], preferred_element_type=F32)
              + jnp.dot(t_lo, rwh_ref[...], preferred_element_type=F32)
              + jnp.dot(t_hi, rwl_ref[...], preferred_element_type=F32)) + rb_ref[...]
    lane = lax.broadcasted_iota(jnp.int32, logits.shape, 1).astype(F32)
    big = 1e9
    gl = jnp.where(lane < N_GROUPS, logits, -jnp.inf)
    gmax = jnp.max(gl, axis=-1, keepdims=True)
    gidx = jnp.min(jnp.where(gl == gmax, lane, big), axis=-1, keepdims=True)
    gw = 1.0 / jnp.sum(jnp.exp(gl - gmax), axis=-1, keepdims=True)
    lo = ROUTER_LANE0 + E_PER_GROUP * gidx
    el = jnp.where((lane >= lo) & (lane < lo + E_PER_GROUP), logits, -jnp.inf)
    v1 = jnp.max(el, axis=-1, keepdims=True)
    i1 = jnp.min(jnp.where(el == v1, lane, big), axis=-1, keepdims=True)
    el2 = jnp.where(lane == i1, -jnp.inf, el)
    v2 = jnp.max(el2, axis=-1, keepdims=True)
    i2 = jnp.min(jnp.where(el2 == v2, lane, big), axis=-1, keepdims=True)
    e2 = jnp.exp(v2 - v1)
    den = 1.0 + e2
    comb = jnp.where(lane == i1, (1.0 / den) * gw, 0.0) + jnp.where(lane == i2, (e2 / den) * gw, 0.0)
    combt_ref[...] = comb.T


def _outproj(merged, x, w_out, g, rwh, rwl, rb):
    n = x.shape[0]
    tm = _tile(n, (256, 128))
    full = lambda shape: pl.BlockSpec(shape, lambda i: (0,) * len(shape))
    row = lambda w: pl.BlockSpec((tm, w), lambda i: (i, 0))
    return pl.pallas_call(
        _outproj_kernel,
        grid=(n // tm,),
        in_specs=[row(D_MODEL), row(D_MODEL), full((D_MODEL, D_MODEL)), full((1, D_MODEL)),
                  full((D_MODEL, LANES)), full((D_MODEL, LANES)), full((1, LANES))],
        out_specs=[row(D_MODEL), row(D_MODEL), pl.BlockSpec((LANES, tm), lambda i: (0, i))],
        out_shape=[jax.ShapeDtypeStruct((n, D_MODEL), F32), jax.ShapeDtypeStruct((n, D_MODEL), BF16),
                   jax.ShapeDtypeStruct((LANES, n), F32)],
        compiler_params=_params("parallel"),
        name="outproj",
    )(merged, x, w_out, g, rwh, rwl, rb)


MOE_CAP = 128


def _moe_kernel(h_ref, ct_ref, x_ref, wg_ref, wu_ref, wd_ref, fg_ref, o_ref, acc_ref, rank_ref, *, final, th):
    e = pl.program_id(1)
    hh = pl.program_id(2)
    rows = pl.ds(pl.multiple_of(hh * th, th), th)

    @pl.when(e == 0)
    def _():
        acc_ref[rows, :] = jnp.zeros((th, D_MODEL), F32)
        sel = jnp.where(ct_ref[...] > 0.0, 1.0, 0.0).astype(BF16)
        before = (lax.broadcasted_iota(jnp.int32, (th, th), 0) < lax.broadcasted_iota(jnp.int32, (th, th), 1))
        rank_ref[hh] = jnp.dot(sel, jnp.where(before, 1.0, 0.0).astype(BF16), preferred_element_type=F32)

    row = ROUTER_LANE0 + e
    w_row = ct_ref[pl.ds(row, 1), :]
    r_row = rank_ref[hh, pl.ds(row, 1), :]
    sel_row = w_row > 0.0
    count = jnp.sum(jnp.where(sel_row, 1.0, 0.0), axis=1, keepdims=True)[0, 0].astype(jnp.int32)
    n_chunks = lax.div(count + (MOE_CAP - 1), MOE_CAP)

    def chunk(s, carry):
        slot = lax.broadcasted_iota(jnp.int32, (MOE_CAP, th), 0).astype(F32) + (s * MOE_CAP).astype(F32)
        pm = (slot == r_row) & sel_row
        pb = jnp.where(pm, 1.0, 0.0).astype(BF16)
        xc = jnp.dot(pb, h_ref[rows, :], preferred_element_type=F32).astype(BF16)
        wc = jnp.sum(jnp.where(pm, w_row, 0.0), axis=1, keepdims=True)
        gate = jnp.dot(xc, wg_ref[0], preferred_element_type=F32)
        up = jnp.dot(xc, wu_ref[0], preferred_element_type=F32)
        hid = (gate * jax.nn.sigmoid(gate) * up * wc).astype(BF16)
        y = jnp.dot(hid, wd_ref[0], preferred_element_type=F32)
        y_hi, y_lo = _split_bf16(y)
        back = lax.dot_general(jnp.concatenate([pb, pb], axis=0), jnp.concatenate([y_hi, y_lo], axis=0),
                               (((0,), (0,)), ((), ())), preferred_element_type=F32)
        acc_ref[rows, :] += back
        return carry

    lax.fori_loop(0, n_chunks, chunk, 0)

    @pl.when(e == pl.num_programs(1) - 1)
    def _():
        y = x_ref[...] + acc_ref[rows, :]
        o_ref[...] = _rms(y, fg_ref[...], EPS) if final else y


def _moe(h2, combt, x, wg, wu, wd, fg, final):
    n = x.shape[0]
    th = _tile(n, (512, 256))
    nh = 2 if n % (2 * th) == 0 else 1
    tb = nh * th
    last = N_EXPERTS - 1
    half = lambda i, e, hh: (nh * i + jnp.where(e == last, hh, 0), 0)
    return pl.pallas_call(
        functools.partial(_moe_kernel, final=final, th=th),
        grid=(n // tb, N_EXPERTS, nh),
        in_specs=[pl.BlockSpec((tb, D_MODEL), lambda i, e, hh: (i, 0)),
                  pl.BlockSpec((LANES, th), lambda i, e, hh: (0, nh * i + hh)),
                  pl.BlockSpec((th, D_MODEL), half),
                  pl.BlockSpec((1, D_MODEL, EXPERT_FF), lambda i, e, hh: (e, 0, 0)),
                  pl.BlockSpec((1, D_MODEL, EXPERT_FF), lambda i, e, hh: (e, 0, 0)),
                  pl.BlockSpec((1, EXPERT_FF, D_MODEL), lambda i, e, hh: (e, 0, 0)),
                  pl.BlockSpec((1, D_MODEL), lambda i, e, hh: (0, 0))],
        out_specs=pl.BlockSpec((th, D_MODEL), half),
        out_shape=jax.ShapeDtypeStruct((n, D_MODEL), F32),
        scratch_shapes=[pltpu.VMEM((tb, D_MODEL), F32), pltpu.VMEM((nh, LANES, th), F32)],
        compiler_params=_params("parallel", "arbitrary", "arbitrary"),
        name="moe",
    )(h2, combt, x, wg, wu, wd, fg)


def _layer_weights(l, norm_mix_g, w_in, mla_q_norm_g, mla_w_uq, mla_kv_norm_g, mla_w_uk, mla_w_uv,
                   conv_w, conv_b, conv_ln_g, conv_ln_b, pool_w, pool_scale,
                   diff_lq1, diff_lk1, diff_lq2, diff_lk2, diff_subln_g,
                   w_branch, w_out, norm_ffn_g, router_g_w, router_g_b, router_e_w, router_e_b,
                   moe_w_gate, moe_w_up, moe_w_down):
    wi = w_in[l]
    o = [0]
    for s in (Q_LORA, KV_LORA, MLA_ROPE, 2 * BRANCH, BRANCH, 512, 256, 256, GATE_COLS):
        o.append(o[-1] + s)
    sl = lambda k: wi[:, o[k]:o[k + 1]]
    glu = sl(3)
    w_mix = jnp.concatenate([sl(0), sl(1), sl(4), glu[:, :BRANCH], glu[:, BRANCH:], sl(5), sl(6), sl(7), sl(2),
                             jnp.zeros((D_MODEL, LANES - MLA_ROPE), F32)], axis=1).astype(BF16)
    uq = mla_w_uq[l]
    half = MLA_ROPE // 2
    wuq = jnp.concatenate([uq[:, :, :MLA_NOPE].reshape(Q_LORA, -1),
                           uq[:, :, MLA_NOPE:MLA_NOPE + half].reshape(Q_LORA, -1),
                           uq[:, :, MLA_NOPE + half:].reshape(Q_LORA, -1)], axis=1).astype(BF16)
    rw = jnp.concatenate([router_g_w[l], jnp.transpose(router_e_w[l], (1, 0, 2)).reshape(D_MODEL, N_EXPERTS),
                          jnp.zeros((D_MODEL, LANES - N_GROUPS - N_EXPERTS), F32)], axis=1)
    rb = jnp.concatenate([router_g_b[l], router_e_b[l].reshape(-1),
                          jnp.zeros((LANES - N_GROUPS - N_EXPERTS,), F32)])[None, :]
    rwh = rw.astype(BF16)
    rwl = (rw - rwh.astype(F32)).astype(BF16)
    return dict(
        norm_mix=norm_mix_g[l][None, :], w_mix=w_mix, w_gate=sl(8).astype(BF16),
        gq=mla_q_norm_g[l][None, :], gkv=mla_kv_norm_g[l][None, :], wuq=wuq,
        wuk=jnp.transpose(mla_w_uk[l], (1, 2, 0)).astype(BF16), wuv=jnp.transpose(mla_w_uv[l], (1, 0, 2)).astype(BF16),
        cw=conv_w[l], cb=conv_b[l][None, :], lg=conv_ln_g[l][None, :], lb=conv_ln_b[l][None, :],
        pw=pool_w[l].astype(BF16), ps=pool_scale[l][None, :],
        lam=(diff_lq1[l][None, :], diff_lk1[l][None, :], diff_lq2[l][None, :], diff_lk2[l][None, :]),
        sub=diff_subln_g[l][None, :],
        wb=w_branch[l].astype(BF16), w_out=w_out[l].astype(BF16), norm_ffn=norm_ffn_g[l][None, :],
        rwh=rwh, rwl=rwl, rb=rb,
        wg=moe_w_gate[l].astype(BF16), wu=moe_w_up[l].astype(BF16), wd=moe_w_down[l].astype(BF16))


def _pool_counts(pos):
    cols = [jnp.broadcast_to(jnp.minimum(w, pos + 1).astype(F32)[:, None], (pos.shape[0], POOL_GROUP))
            for w in POOL_WINDOWS]
    return jnp.concatenate(cols, axis=1)


def _tile_rows(tab, rows):
    return jnp.tile(tab, (rows // tab.shape[0], 1)) if tab.shape[0] < rows else tab


def _trunk(x, pos, b, t, prev_conv, prev_pool, attend, weights, final_g, depth):
    tab_rows = t if t >= 64 else 64 * t
    tabs = tuple(_tile_rows(tb, tab_rows) for tb in _rope_tables(pos))
    cnt = _pool_counts(pos)
    news = []
    for l in range(depth):
        p = weights[l]
        lam_init = 0.8 - 0.6 * math.exp(-0.3 * l)
        zmix, h = _inproj(x, p['norm_mix'], p['w_mix'])
        gates = _gates(h, p['w_gate'])
        qmla, rows, dqr, dkr = _prep(zmix, p['gq'], p['gkv'], p['wuq'], p['wuk'], tabs, attend.q_dtype)
        a_out, d_out = attend(l, p, qmla, rows, dqr, dkr, zmix, lam_init)
        pc = jnp.pad(prev_conv[l], ((0, 0), (CONV_HALO - CONV_PREV, 0), (0, 0)))
        pp = jnp.pad(prev_pool[l], ((0, 0), (POOL_HALO - POOL_PREV, 0), (0, 0)))
        b_out, c_out, new_conv, new_pool = _seq_mixers(zmix, pc, pp, p['cw'], p['cb'], p['lg'], p['lb'],
                                                       p['pw'], p['ps'], cnt, b, t)
        merged = _merge(gates, (a_out, b_out, c_out, d_out), p['wb'])
        x, h2, combt = _outproj(merged, x, p['w_out'], p['norm_ffn'], p['rwh'], p['rwl'], p['rb'])
        x = _moe(h2, combt, x, p['wg'], p['wu'], p['wd'], final_g, final=(l == depth - 1))
        news.append((rows.reshape(b, t, MLA_ROW), dkr.reshape(b, t, DIFF_KV, 2, DIFF_DH),
                     zmix[:, MIX_DV:MIX_DV + 256].reshape(b, t, DIFF_KV, DIFF_VD), new_conv, new_pool))
    stacked = [jnp.stack([nw[i] for nw in news], axis=0) for i in range(5)]
    return x.reshape(b, t, D_MODEL), stacked


class _PromptAttend:
    q_dtype = BF16

    def __init__(self, b, t):
        self.b, self.t = b, t

    def __call__(self, l, p, qmla, rows, dqr, dkr, zmix, lam_init):
        a_out = _mla_prompt(qmla, rows, p['wuv'], self.b, self.t)
        d_out = _diff_prompt(dqr, dkr, zmix, p['lam'], p['sub'], self.b, self.t, lam_init)
        return a_out, d_out


class _PagedAttend:
    q_dtype = F32

    def __init__(self, page_table, cache_mla, cache_k, cache_v, tn):
        depth, n_pool, ps = cache_mla.shape[:3]
        self.pt, self.tn = page_table, tn
        self.cm = jnp.transpose(cache_mla, (0, 1, 3, 2))
        self.ck = jnp.transpose(cache_k, (0, 1, 3, 4, 5, 2)).reshape(depth, n_pool, 2 * DIFF_KV * DIFF_DH, ps)
        self.cv = cache_v.reshape(depth, n_pool, ps * DIFF_KV, DIFF_VD)

    def __call__(self, l, p, qmla, rows, dqr, dkr, zmix, lam_init):
        a_out = _mla_paged(self.pt, qmla, rows, p['wuv'], self.cm, l, self.tn)
        d_out = _diff_paged(self.pt, dqr, dkr, zmix, p['lam'], p['sub'], self.ck, self.cv, l, self.tn, lam_init)
        return a_out, d_out


def kernel(x_prompt, x_sample, cache_mla_kv, cache_diff_k, cache_diff_v, state_conv, state_pool, page_table, norm_mix_g, w_in, mla_q_norm_g, mla_w_uq, mla_kv_norm_g, mla_w_uk, mla_w_uv, conv_w, conv_b, conv_ln_g, conv_ln_b, pool_w, pool_scale, diff_lq1, diff_lk1, diff_lq2, diff_lk2, diff_subln_g, w_branch, w_out, norm_ffn_g, router_g_w, router_g_b, router_e_w, router_e_b, moe_w_gate, moe_w_up, moe_w_down, final_norm_g):
    depth = w_in.shape[0]
    weights = [_layer_weights(l, norm_mix_g, w_in, mla_q_norm_g, mla_w_uq, mla_kv_norm_g, mla_w_uk, mla_w_uv,
                              conv_w, conv_b, conv_ln_g, conv_ln_b, pool_w, pool_scale,
                              diff_lq1, diff_lk1, diff_lq2, diff_lk2, diff_subln_g,
                              w_branch, w_out, norm_ffn_g, router_g_w, router_g_b, router_e_w, router_e_b,
                              moe_w_gate, moe_w_up, moe_w_down) for l in range(depth)]
    final_g = final_norm_g[None, :]

    bp, tp, _ = x_prompt.shape
    pos_p = jnp.arange(tp, dtype=jnp.int32)
    conv0 = jnp.zeros((depth, bp, CONV_PREV, BRANCH), F32)
    pool0 = jnp.zeros((depth, bp, POOL_PREV, BRANCH), F32)
    y_p, news_p = _trunk(x_prompt.reshape(bp * tp, D_MODEL), pos_p, bp, tp, conv0, pool0,
                         _PromptAttend(bp, tp), weights, final_g, depth)

    bs, ts, _ = x_sample.shape
    past_len = page_table.shape[1] * cache_mla_kv.shape[2]
    pos_s = past_len + jnp.arange(ts, dtype=jnp.int32)
    y_s, news_s = _trunk(x_sample.reshape(bs * ts, D_MODEL), pos_s, bs, ts, state_conv, state_pool,
                         _PagedAttend(page_table, cache_mla_kv, cache_diff_k, cache_diff_v, ts), weights, final_g, depth)

    return (y_p, y_s, *news_p, *news_s)
```
